```python
import jax, jax.numpy as jnp
from jax import lax
import numpy as np

D_MODEL = 2048
BATCH = 4
SEQ = 2048
DEPTH = 4

N_MIXERS = 2
NORM_EPS = 1e-6
GLA_HEADS = 4
GLA_KEY_DIM = D_MODEL // 2
GLA_VALUE_DIM = D_MODEL
GLA_DK = GLA_KEY_DIM // GLA_HEADS
GLA_DV = GLA_VALUE_DIM // GLA_HEADS
GLA_GATE_RANK = 16
GLA_GATE_NORMALIZER = 16.0
GLA_CHUNK = 64
GLA_IN_WIDTH = 2 * GLA_KEY_DIM + 2 * GLA_VALUE_DIM + GLA_GATE_RANK
MOBA_HEADS = 16
MOBA_HEAD_DIM = D_MODEL // MOBA_HEADS
MOBA_BLOCK = 256
MOBA_TOPK = 3
MOBA_QUERY_CHUNK = 16
ROPE_THETA = 10000.0
N_EXPERTS = 64
MOE_TOPK = 6
EXPERT_DIM = 384
SHARED_DIM = 384
N_GROUPS = 8
TOPK_GROUPS = 4
ROUTED_SCALE = 2.5
MOE_BLOCK = 256
N_GLA_LAYERS = (DEPTH + N_MIXERS - 1) // N_MIXERS
N_MOBA_LAYERS = DEPTH // N_MIXERS

kernel_name = 'hybrid_gla_moba_moe_adaln_trunk'

F32 = jnp.float32


def rms_norm(t, w):
    tf = t.astype(F32)
    tf = tf * lax.rsqrt(jnp.mean(tf * tf, axis=-1, keepdims=True) + NORM_EPS)
    return (tf * w.astype(F32)).astype(t.dtype)


def apply_rope(t, pos):
    hd = t.shape[-1]
    half = hd // 2
    inv = jnp.power(ROPE_THETA, -jnp.arange(half, dtype=F32) * 2.0 / hd)
    ang = pos.astype(F32)[:, None] * inv[None, :]
    cos, sin = jnp.cos(ang), jnp.sin(ang)
    tf = t.astype(F32)
    t1, t2 = tf[..., :half], tf[..., half:]
    return jnp.concatenate([t1 * cos - t2 * sin, t2 * cos + t1 * sin], axis=-1).astype(t.dtype)


def gla_mixer(h, w_in, w_gk_up, b_gk, norm_w, w_out):
    B, S, _ = h.shape
    H, dk, dv, L = GLA_HEADS, GLA_DK, GLA_DV, GLA_CHUNK
    n = S // L
    proj = h @ w_in
    q, k, v, g, gk_low = jnp.split(proj, [GLA_KEY_DIM, 2 * GLA_KEY_DIM, 2 * GLA_KEY_DIM + GLA_VALUE_DIM, 2 * GLA_KEY_DIM + 2 * GLA_VALUE_DIM], axis=-1)
    log_a = jax.nn.log_sigmoid((gk_low @ w_gk_up + b_gk).astype(F32)) / GLA_GATE_NORMALIZER

    def chunked(t, d):
        return t.astype(F32).reshape(B, n, L, H, d).transpose(0, 3, 1, 2, 4)

    q = chunked(q, dk) * (dk ** -0.5)
    k = chunked(k, dk)
    v = chunked(v, dv)
    log_a = chunked(log_a, dk)
    b = jnp.cumsum(log_a, axis=3)
    b_last = b[:, :, :, -1:, :]
    q_e = q * jnp.exp(b)
    k_e = k * jnp.exp(-b)
    k_t = k * jnp.exp(b_last - b)
    causal = jnp.tril(jnp.ones((L, L), dtype=bool))
    att = jnp.where(causal, jnp.einsum('bhnid,bhnjd->bhnij', q_e, k_e), 0.0)
    o_intra = jnp.einsum('bhnij,bhnjv->bhniv', att, v)

    def step(state, inp):
        qe_c, kt_c, v_c, decay_c = inp
        o_c = jnp.einsum('bhld,bhdv->bhlv', qe_c, state)
        state = decay_c[..., None] * state + jnp.einsum('bhld,bhlv->bhdv', kt_c, v_c)
        return state, o_c

    xs = (jnp.moveaxis(q_e, 2, 0), jnp.moveaxis(k_t, 2, 0), jnp.moveaxis(v, 2, 0),
          jnp.moveaxis(jnp.exp(b_last[:, :, :, 0, :]), 2, 0))
    state0 = jnp.zeros((B, H, dk, dv), F32)
    _, o_inter = lax.scan(step, state0, xs)
    o = o_intra + jnp.moveaxis(o_inter, 0, 2)
    o = o.transpose(0, 2, 3, 1, 4).reshape(B, S, H, dv)
    o = rms_norm(o, norm_w) * jax.nn.silu(g.astype(F32).reshape(B, S, H, dv))
    return o.reshape(B, S, GLA_VALUE_DIM).astype(h.dtype) @ w_out


def moba_mixer(h, w_qkv, w_out):
    B, S, _ = h.shape
    H, hd, BLK, QC = MOBA_HEADS, MOBA_HEAD_DIM, MOBA_BLOCK, MOBA_QUERY_CHUNK
    qkv = (h @ w_qkv).reshape(B, S, 3, H, hd)
    q = qkv[:, :, 0].transpose(0, 2, 1, 3)
    k = qkv[:, :, 1].transpose(0, 2, 1, 3)
    v = qkv[:, :, 2].transpose(0, 2, 1, 3)
    pos = jnp.arange(S)
    q = apply_rope(q, pos)
    k = apply_rope(k, pos)
    nb = -(-S // BLK)
    s_pad = nb * BLK
    padw = ((0, 0), (0, 0), (0, s_pad - S), (0, 0))
    q, k, v = jnp.pad(q, padw), jnp.pad(k, padw), jnp.pad(v, padw)
    kb = k.reshape(B, H, nb, BLK, hd)
    vb = v.reshape(B, H, nb, BLK, hd)
    n_sel = min(MOBA_TOPK, nb - 1)
    n_chunks = s_pad // QC
    qblk = jnp.arange(s_pad) // BLK

    def to_chunks(t):
        return jnp.moveaxis(t.reshape(B, H, n_chunks, QC, *t.shape[3:]), 2, 0)

    xs = (jnp.arange(n_chunks), to_chunks(q))
    if n_sel > 0:
        k_mean = kb.astype(F32).mean(axis=3)
        gate = jnp.einsum('bhsd,bhnd->bhsn', q.astype(F32), k_mean)
        past = jnp.arange(nb)[None, :] < qblk[:, None]
        gate = jnp.where(past, gate, -jnp.inf)
        _, sel_idx = lax.top_k(gate, n_sel)
        sel_valid = jnp.arange(n_sel)[None, :] < qblk[:, None]
        xs = xs + (to_chunks(sel_idx), sel_valid.reshape(n_chunks, QC, n_sel))
    b_ix = jnp.arange(B)[:, None, None, None]
    h_ix = jnp.arange(H)[None, :, None, None]
    scale = hd ** -0.5

    def attend(inp):
        cid, qc = inp[0], inp[1]
        start = cid * QC
        qpos = start + jnp.arange(QC)
        ob = start // BLK
        k_own = lax.dynamic_index_in_dim(kb, ob, axis=2, keepdims=False)
        v_own = lax.dynamic_index_in_dim(vb, ob, axis=2, keepdims=False)
        kpos = ob * BLK + jnp.arange(BLK)
        s_own = jnp.einsum('bhqd,bhkd->bhqk', qc, k_own).astype(F32) * scale
        s_own = jnp.where(kpos[None, :] <= qpos[:, None], s_own, -jnp.inf)
        if n_sel == 0:
            p = jax.nn.softmax(s_own, axis=-1).astype(v.dtype)
            return jnp.einsum('bhqk,bhkd->bhqd', p, v_own)
        idx, valid = inp[2], inp[3]
        k_sel = kb[b_ix, h_ix, idx]
        v_sel = vb[b_ix, h_ix, idx]
        s_sel = jnp.einsum('bhqd,bhqnkd->bhqnk', qc, k_sel).astype(F32) * scale
        s_sel = jnp.where(valid[None, None, :, :, None], s_sel, -jnp.inf)
        s_all = jnp.concatenate([s_sel.reshape(B, H, QC, n_sel * BLK), s_own], axis=-1)
        p = jax.nn.softmax(s_all, axis=-1).astype(v.dtype)
        p_sel = p[..., :n_sel * BLK].reshape(B, H, QC, n_sel, BLK)
        p_own = p[..., n_sel * BLK:]
        return (jnp.einsum('bhqnk,bhqnkd->bhqd', p_sel, v_sel)
                + jnp.einsum('bhqk,bhkd->bhqd', p_own, v_own))

    o = lax.map(attend, xs)
    o = jnp.moveaxis(o, 0, 2).reshape(B, H, s_pad, hd)[:, :, :S]
    o = o.transpose(0, 2, 1, 3).reshape(B, S, H * hd)
    return o @ w_out


def swiglu(t, w_gate_up, w_down):
    gate, up = jnp.split(t @ w_gate_up, 2, axis=-1)
    return (jax.nn.silu(gate) * up) @ w_down


def route(hf, router_w, router_bias):
    T = hf.shape[0]
    scores = jax.nn.sigmoid((hf @ router_w).astype(F32))
    biased = scores + router_bias.astype(F32)
    grouped = biased.reshape(T, N_GROUPS, N_EXPERTS // N_GROUPS)
    group_score = lax.top_k(grouped, 2)[0].sum(axis=-1)
    _, top_groups = lax.top_k(group_score, TOPK_GROUPS)
    group_keep = jax.nn.one_hot(top_groups, N_GROUPS, dtype=F32).sum(axis=-2) > 0
    expert_keep = jnp.repeat(group_keep, N_EXPERTS // N_GROUPS, axis=-1)
    _, idx = lax.top_k(jnp.where(expert_keep, biased, -jnp.inf), MOE_TOPK)
    w = jnp.take_along_axis(scores, idx, axis=-1)
    w = w / jnp.sum(w, axis=-1, keepdims=True) * ROUTED_SCALE
    return idx, w


def moe_ffn(h, router_w, router_bias, w_gate_up, w_down, sh_gate_up, sh_down):
    B, S, D = h.shape
    hf = h.reshape(B * S, D)
    T = hf.shape[0]
    idx, gw = route(hf, router_w, router_bias)
    shared = swiglu(hf, sh_gate_up, sh_down)
    A = T * MOE_TOPK
    flat_e = idx.reshape(A)
    flat_w = gw.reshape(A).astype(hf.dtype)
    flat_t = jnp.arange(A, dtype=jnp.int32) // MOE_TOPK
    order = jnp.argsort(flat_e, stable=True)
    se, st, sw = flat_e[order], flat_t[order], flat_w[order]
    counts = jnp.bincount(flat_e, length=N_EXPERTS)
    padded = (counts + MOE_BLOCK - 1) // MOE_BLOCK * MOE_BLOCK
    pad_end = jnp.cumsum(padded)
    pad_start = pad_end - padded
    grp_start = jnp.cumsum(counts) - counts
    dest = pad_start[se] + jnp.arange(A, dtype=jnp.int32) - grp_start[se]
    n_blocks = -(-(A + N_EXPERTS * (MOE_BLOCK - 1)) // MOE_BLOCK)
    P = n_blocks * MOE_BLOCK
    slot_t = jnp.zeros((P,), jnp.int32).at[dest].set(st)
    slot_w = jnp.zeros((P,), hf.dtype).at[dest].set(sw)
    blk_e = jnp.minimum(jnp.searchsorted(pad_end, jnp.arange(n_blocks) * MOE_BLOCK, side='right'), N_EXPERTS - 1)

    def step(acc, inp):
        tok, wt, e = inp
        xb = hf[tok]
        gate, up = jnp.split(xb @ w_gate_up[e], 2, axis=-1)
        y = (jax.nn.silu(gate) * up) @ w_down[e]
        return acc.at[tok].add(y * wt[:, None]), None

    routed, _ = lax.scan(step, jnp.zeros_like(hf),
                         (slot_t.reshape(n_blocks, MOE_BLOCK), slot_w.reshape(n_blocks, MOE_BLOCK), blk_e))
    return (shared + routed).reshape(B, S, D)


def setup_inputs(seed: int = 0) -> dict:
    key = jax.random.key(seed)
    ks = jax.random.split(key, 18)
    D = D_MODEL
    L, NG, NM = DEPTH, N_GLA_LAYERS, N_MOBA_LAYERS

    def nrm(k, shape, scale):
        return jax.random.normal(k, shape, F32) * scale

    return {
        'x': nrm(ks[0], (BATCH, SEQ, D), 1.0),
        'c': nrm(ks[1], (BATCH, D), 1.0),
        'ada_w': nrm(ks[2], (L, D, 6 * D), 0.3 * D ** -0.5),
        'ada_b': nrm(ks[3], (L, 6 * D), 0.02),
        'norm_w': 1.0 + nrm(ks[4], (L, 4, D), 0.05),
        'gla_w_in': nrm(ks[5], (NG, D, GLA_IN_WIDTH), D ** -0.5),
        'gla_w_gk_up': nrm(ks[6], (NG, GLA_GATE_RANK, GLA_KEY_DIM), GLA_GATE_RANK ** -0.5),
        'gla_b_gk': nrm(ks[7], (NG, GLA_KEY_DIM), 0.02),
        'gla_norm_w': 1.0 + nrm(ks[8], (NG, GLA_DV), 0.05),
        'gla_w_out': nrm(ks[9], (NG, GLA_VALUE_DIM, D), GLA_VALUE_DIM ** -0.5),
        'moba_w_qkv': nrm(ks[10], (NM, D, 3 * D), D ** -0.5),
        'moba_w_out': nrm(ks[11], (NM, D, D), D ** -0.5),
        'router_w': nrm(ks[12], (L, D, N_EXPERTS), D ** -0.5),
        'router_bias': nrm(ks[13], (L, N_EXPERTS), 0.01),
        'moe_w_gate_up': nrm(ks[14], (L, N_EXPERTS, D, 2 * EXPERT_DIM), D ** -0.5),
        'moe_w_down': nrm(ks[15], (L, N_EXPERTS, EXPERT_DIM, D), EXPERT_DIM ** -0.5),
        'shared_w_gate_up': nrm(ks[16], (L, D, 2 * SHARED_DIM), D ** -0.5),
        'shared_w_down': nrm(ks[17], (L, SHARED_DIM, D), SHARED_DIM ** -0.5),
    }


def reference(x, c, ada_w, ada_b, norm_w, gla_w_in, gla_w_gk_up, gla_b_gk, gla_norm_w, gla_w_out,
              moba_w_qkv, moba_w_out, router_w, router_bias, moe_w_gate_up, moe_w_down,
              shared_w_gate_up, shared_w_down):
    cond = jax.nn.silu(c)
    for i in range(DEPTH):
        mod = (cond @ ada_w[i] + ada_b[i])[:, None, :]
        shift_m, scale_m, gate_m, shift_f, scale_f, gate_f = jnp.split(mod, 6, axis=-1)
        h = rms_norm(x, norm_w[i, 0]) * (1 + scale_m) + shift_m
        j = i // N_MIXERS
        if i % N_MIXERS == 0:
            y = gla_mixer(h, gla_w_in[j], gla_w_gk_up[j], gla_b_gk[j], gla_norm_w[j], gla_w_out[j])
        else:
            y = moba_mixer(h, moba_w_qkv[j], moba_w_out[j])
        x = x + gate_m * rms_norm(y, norm_w[i, 1])
        h = rms_norm(x, norm_w[i, 2]) * (1 + scale_f) + shift_f
        y = moe_ffn(h, router_w[i], router_bias[i], moe_w_gate_up[i], moe_w_down[i],
                    shared_w_gate_up[i], shared_w_down[i])
        x = x + gate_f * rms_norm(y, norm_w[i, 3])
    return x
```

```python
import functools

import jax
import jax.numpy as jnp
from jax import lax
from jax.experimental import pallas as pl
from jax.experimental.pallas import tpu as pltpu

F32 = jnp.float32
BF16 = jnp.bfloat16
U32 = jnp.uint32
I32 = jnp.int32

NORM_EPS = 1e-6
N_MIXERS = 2
GLA_HEADS = 4
GLA_GATE_RANK = 16
GLA_GATE_NORMALIZER = 16.0
GLA_CHUNK = 64
MOBA_HEAD_DIM = 128
MOBA_BLOCK = 256
MOBA_TOPK = 3
ROPE_THETA = 10000.0
MOE_TOPK = 6
N_GROUPS = 8
TOPK_GROUPS = 4
ROUTED_SCALE = 2.5

LANES = 128
SUBLANES = 8
MOE_ROWS = 256
ROUTE_LANES = LANES
TOPK_PAD = 8
VMEM_LIMIT = 56 * 1024 * 1024

NEG_INF = float("-inf")
HI_MASK = 0xFFFF0000


def _cparams(sem):
    return pltpu.CompilerParams(dimension_semantics=sem, vmem_limit_bytes=VMEM_LIMIT)


def _sigmoid(x):
    return 1.0 / (1.0 + jnp.exp(-x))


def _silu(x):
    return x * _sigmoid(x)


def _nt(a, b):
    return lax.dot_general(a, b, (((1,), (1,)), ((), ())), preferred_element_type=F32)


def _tn(a, b):
    return lax.dot_general(a, b, (((0,), (0,)), ((), ())), preferred_element_type=F32)


def _dot(a, b):
    return jnp.dot(a, b, preferred_element_type=F32)


def _split_bf16(x):
    hi = x.astype(BF16)
    lo = (x - hi.astype(F32)).astype(BF16)
    return hi, lo


def _pack_rows(y):
    w = y.shape[1] // 2
    bits = pltpu.bitcast(y.astype(BF16).astype(F32), U32)
    return (bits[:, :w] >> 16) | (bits[:, w:] & jnp.uint32(HI_MASK))


def _unpack_rows(w):
    lo = pltpu.bitcast(w << 16, F32)
    hi = pltpu.bitcast(w & jnp.uint32(HI_MASK), F32)
    return jnp.concatenate([lo, hi], axis=1)


def _rms(x):
    return x * lax.rsqrt(jnp.mean(x * x, axis=-1, keepdims=True) + NORM_EPS)


def _ada_kernel(c_ref, w_ref, b_ref, o_ref):
    cond = _silu(c_ref[...]).astype(BF16)
    o_ref[0] = _dot(cond, w_ref[0].astype(BF16)) + b_ref[0]


def _ada_mod(c, ada_w, ada_b):
    L, D, N = ada_w.shape
    B = c.shape[0]
    tn = 1024
    return pl.pallas_call(
        _ada_kernel,
        grid=(L, N // tn),
        in_specs=[pl.BlockSpec((B, D), lambda l, j: (0, 0)),
                  pl.BlockSpec((1, D, tn), lambda l, j: (l, 0, j)),
                  pl.BlockSpec((1, 1, tn), lambda l, j: (l, 0, j))],
        out_specs=pl.BlockSpec((1, B, tn), lambda l, j: (l, 0, j)),
        out_shape=jax.ShapeDtypeStruct((L, B, N), F32),
        compiler_params=_cparams(("arbitrary", "arbitrary")),
        name="ada_mod",
    )(c, ada_w, ada_b.reshape(L, 1, N))


def _prenorm(x, nw, scale, shift):
    return _rms(x) * nw * (1.0 + scale) + shift


def _normproj_kernel(x_ref, nw_ref, sc_ref, sh_ref, w_ref, *rest, with_gk):
    if with_gk:
        wg_ref, o_ref, og_ref, h_scr = rest
    else:
        o_ref, h_scr = rest

    @pl.when(pl.program_id(1) == 0)
    def _():
        h = _prenorm(x_ref[...], nw_ref[...], sc_ref[0], sh_ref[0])
        h_scr[...] = h.astype(BF16)
        if with_gk:
            og_ref[...] = _dot(h_scr[...], wg_ref[...])

    o_ref[...] = _dot(h_scr[...], w_ref[0].astype(BF16)).astype(o_ref.dtype)


def _norm_proj(x2, nw, mod3, sc_row, sh_row, w3, widx, n_main, seq, w_gk=None):
    T, D = x2.shape
    tm, tn = min(1024, seq), 512
    per_b = seq // tm
    mod_spec = lambda row: pl.BlockSpec((1, 1, D), lambda i, j: (row(i // per_b), 0, 0))
    in_specs = [pl.BlockSpec((tm, D), lambda i, j: (i, 0)),
                pl.BlockSpec((1, D), lambda i, j: (0, 0)),
                mod_spec(sc_row), mod_spec(sh_row),
                pl.BlockSpec((1, D, tn), lambda i, j: (widx, 0, j))]
    out_specs = [pl.BlockSpec((tm, tn), lambda i, j: (i, j))]
    out_shape = [jax.ShapeDtypeStruct((T, n_main), BF16)]
    args = [x2, nw, mod3, mod3, w3]
    if w_gk is not None:
        in_specs.append(pl.BlockSpec((D, LANES), lambda i, j: (0, 0)))
        out_specs.append(pl.BlockSpec((tm, LANES), lambda i, j: (i, 0)))
        out_shape.append(jax.ShapeDtypeStruct((T, LANES), F32))
        args.append(w_gk)
    return pl.pallas_call(
        functools.partial(_normproj_kernel, with_gk=w_gk is not None),
        grid=(T // tm, n_main // tn),
        in_specs=in_specs, out_specs=out_specs, out_shape=out_shape,
        scratch_shapes=[pltpu.VMEM((tm, D), BF16)],
        compiler_params=_cparams(("arbitrary", "arbitrary")),
        name="norm_proj",
    )(*args)


def _outproj_kernel(a_ref, w_ref, x_ref, nw_ref, g_ref, o_ref):
    y = _dot(a_ref[...], w_ref[...])
    o_ref[...] = x_ref[...] + g_ref[0] * (_rms(y) * nw_ref[...])


def _out_proj(a, w_bf, x2, nw, mod3, gate_row, seq):
    T, D = x2.shape
    tm = min(512, seq)
    per_b = seq // tm
    return pl.pallas_call(
        _outproj_kernel,
        grid=(T // tm,),
        in_specs=[pl.BlockSpec((tm, a.shape[1]), lambda i: (i, 0)),
                  pl.BlockSpec(w_bf.shape, lambda i: (0, 0)),
                  pl.BlockSpec((tm, D), lambda i: (i, 0)),
                  pl.BlockSpec((1, D), lambda i: (0, 0)),
                  pl.BlockSpec((1, 1, D), lambda i: (gate_row(i // per_b), 0, 0))],
        out_specs=pl.BlockSpec((tm, D), lambda i: (i, 0)),
        out_shape=jax.ShapeDtypeStruct((T, D), F32),
        compiler_params=_cparams(("arbitrary",)),
        name="out_proj",
    )(a, w_bf, x2, nw, mod3)


def _gla_kernel(q_ref, k_ref, v_ref, g_ref, gk_ref, wup_ref, bgk_ref, nw_ref, o_ref, st_ref, *, n_chunk, dk):
    L = GLA_CHUNK

    @pl.when(pl.program_id(2) == 0)
    def _():
        st_ref[...] = jnp.zeros_like(st_ref)

    z = _dot(gk_ref[...], wup_ref[...]) + bgk_ref[...]
    log_a = (jnp.minimum(z, 0.0) - jnp.log(1.0 + jnp.exp(-jnp.abs(z)))) / GLA_GATE_NORMALIZER
    row = lax.broadcasted_iota(I32, (L, L), 0)
    col = lax.broadcasted_iota(I32, (L, L), 1)
    causal = row >= col
    tri = jnp.where(causal, 1.0, 0.0).astype(BF16)
    scale = dk ** -0.5
    for c in range(n_chunk):
        sl = pl.ds(c * L, L)
        la_hi, la_lo = _split_bf16(log_a[c * L:(c + 1) * L])
        b = _dot(tri, la_hi) + _dot(tri, la_lo)
        b_last = b[L - 1:L]
        q = q_ref[sl, :].astype(F32) * scale
        k = k_ref[sl, :].astype(F32)
        v = v_ref[sl, :]
        qe = (q * jnp.exp(b)).astype(BF16)
        ke = (k * jnp.exp(-b)).astype(BF16)
        kt = (k * jnp.exp(b_last - b)).astype(BF16)
        att = jnp.where(causal, _nt(qe, ke), 0.0)
        st = st_ref[...]
        o = _dot(att.astype(BF16), v) + _nt(qe, st.astype(BF16))
        st_ref[...] = st * jnp.exp(b_last) + _tn(v, kt)
        g = g_ref[sl, :].astype(F32)
        o_ref[sl, :] = (_rms(o) * nw_ref[...] * _silu(g)).astype(o_ref.dtype)


def _gla_core(proj, gk, wup_pad, bgk, gnw, batch, seq, key_dim, val_dim):
    T = proj.shape[0]
    H = GLA_HEADS
    dk, dv = key_dim // H, val_dim // H
    R = min(512, seq)
    nr = seq // R
    row = lambda b, h, r: b * nr + r
    return pl.pallas_call(
        functools.partial(_gla_kernel, n_chunk=R // GLA_CHUNK, dk=dk),
        grid=(batch, H, nr),
        in_specs=[pl.BlockSpec((R, dk), lambda b, h, r: (row(b, h, r), h)),
                  pl.BlockSpec((R, dk), lambda b, h, r: (row(b, h, r), H + h)),
                  pl.BlockSpec((R, dv), lambda b, h, r: (row(b, h, r), 2 * key_dim // dv + h)),
                  pl.BlockSpec((R, dv), lambda b, h, r: (row(b, h, r), (2 * key_dim + val_dim) // dv + h)),
                  pl.BlockSpec((R, LANES), lambda b, h, r: (row(b, h, r), 0)),
                  pl.BlockSpec((LANES, dk), lambda b, h, r: (0, h)),
                  pl.BlockSpec((1, dk), lambda b, h, r: (0, h)),
                  pl.BlockSpec((1, dv), lambda b, h, r: (0, 0))],
        out_specs=pl.BlockSpec((R, dv), lambda b, h, r: (row(b, h, r), h)),
        out_shape=jax.ShapeDtypeStruct((T, val_dim), BF16),
        scratch_shapes=[pltpu.VMEM((dv, dk), F32)],
        compiler_params=_cparams(("arbitrary", "arbitrary", "arbitrary")),
        name="gla_core",
    )(proj, proj, proj, proj, gk, wup_pad, bgk, gnw)


def _moba_kernel(q_ref, k_ref, v_ref, cos_ref, sin_ref, o_ref, *, nb):
    blk, hd = MOBA_BLOCK, MOBA_HEAD_DIM
    S = nb * blk
    cos, sin = cos_ref[...], sin_ref[...]

    def rope(t_ref):
        t = t_ref[...].astype(F32)
        return t * cos + pltpu.roll(t, hd // 2, 1) * sin

    q = rope(q_ref) * (hd ** -0.5)
    k = rope(k_ref)
    qb, kb = q.astype(BF16), k.astype(BF16)
    vT = v_ref[...].astype(F32).T.astype(BF16)

    n_sel = min(MOBA_TOPK, nb - 1)
    if n_sel > 0:
        kmean = jnp.concatenate(
            [jnp.mean(k[j * blk:(j + 1) * blk], axis=0, keepdims=True) for j in range(nb)], axis=0)
        jrow = lax.broadcasted_iota(I32, (nb, S), 0)
        qblk = lax.shift_right_logical(lax.broadcasted_iota(I32, (nb, S), 1), blk.bit_length() - 1)
        past = jrow < qblk
        gate = jnp.where(past, _nt(kmean, q), NEG_INF)
        rank = jnp.zeros((nb, S), F32)
        for j2 in range(nb):
            r = gate[j2:j2 + 1]
            beats = jnp.where(r > gate, 1.0, jnp.where((r == gate) & (j2 < jrow), 1.0, 0.0))
            rank = rank + beats
        sel = jnp.where(past & (rank < n_sel), 1.0, 0.0)

    kk = lax.broadcasted_iota(I32, (blk, blk), 0)
    qq = lax.broadcasted_iota(I32, (blk, blk), 1)
    for i in range(nb):
        qi = qb[i * blk:(i + 1) * blk]
        s = jnp.where(kk <= qq, _nt(kb[i * blk:(i + 1) * blk], qi), NEG_INF)
        m = jnp.max(s, axis=0, keepdims=True)
        p = jnp.exp(s - m)
        l = jnp.sum(p, axis=0, keepdims=True)
        acc = _dot(vT[:, i * blk:(i + 1) * blk], p.astype(BF16))
        for j in range(i if n_sel > 0 else 0):
            keep = sel[j:j + 1, i * blk:(i + 1) * blk] > 0.0
            s = jnp.where(keep, _nt(kb[j * blk:(j + 1) * blk], qi), NEG_INF)
            m_new = jnp.maximum(m, jnp.max(s, axis=0, keepdims=True))
            alpha = jnp.exp(m - m_new)
            p = jnp.exp(s - m_new)
            l = alpha * l + jnp.sum(p, axis=0, keepdims=True)
            acc = alpha * acc + _dot(vT[:, j * blk:(j + 1) * blk], p.astype(BF16))
            m = m_new
        o_ref[pl.ds(i * blk, blk), :] = (acc / l).T.astype(o_ref.dtype)


def _moba_core(qkv, cos, sin, batch, seq, d_model):
    T = qkv.shape[0]
    hd = MOBA_HEAD_DIM
    H = d_model // hd
    nb = seq // MOBA_BLOCK
    return pl.pallas_call(
        functools.partial(_moba_kernel, nb=nb),
        grid=(batch, H),
        in_specs=[pl.BlockSpec((seq, hd), lambda b, h: (b, h)),
                  pl.BlockSpec((seq, hd), lambda b, h: (b, H + h)),
                  pl.BlockSpec((seq, hd), lambda b, h: (b, 2 * H + h)),
                  pl.BlockSpec((seq, hd), lambda b, h: (0, 0)),
                  pl.BlockSpec((seq, hd), lambda b, h: (0, 0))],
        out_specs=pl.BlockSpec((seq, hd), lambda b, h: (b, h)),
        out_shape=jax.ShapeDtypeStruct((T, d_model), BF16),
        compiler_params=_cparams(("arbitrary", "arbitrary")),
        name="moba_core",
    )(qkv, qkv, qkv, cos, sin)


def _rope_tables(seq):
    hd = MOBA_HEAD_DIM
    half = hd // 2
    inv = jnp.power(ROPE_THETA, -jnp.arange(half, dtype=F32) * 2.0 / hd)
    ang = jnp.arange(seq).astype(F32)[:, None] * inv[None, :]
    cos, sin = jnp.cos(ang), jnp.sin(ang)
    return jnp.concatenate([cos, cos], axis=-1), jnp.concatenate([-sin, sin], axis=-1)


def _route_chunk(lt, bias, carry, n_exp):
    C = lt.shape[1]
    per_g = n_exp // N_GROUPS
    scores = _sigmoid(lt)
    biased = scores + bias
    sub = lax.broadcasted_iota(I32, (per_g, C), 0).astype(F32)
    gs_rows, groups = [], []
    for g in range(N_GROUPS):
        xg = biased[g * per_g:(g + 1) * per_g]
        groups.append(xg)
        m1 = jnp.max(xg, axis=0, keepdims=True)
        i1 = jnp.min(jnp.where(xg == m1, sub, float(per_g)), axis=0, keepdims=True)
        m2 = jnp.max(jnp.where(sub == i1, NEG_INF, xg), axis=0, keepdims=True)
        gs_rows.append(m1 + m2)
    gs = jnp.concatenate(gs_rows, axis=0)
    gidx = lax.broadcasted_iota(I32, (N_GROUPS, C), 0)
    grank = jnp.zeros((N_GROUPS, C), F32)
    for g2 in range(N_GROUPS):
        r = gs[g2:g2 + 1]
        grank = grank + jnp.where(r > gs, 1.0, jnp.where((r == gs) & (g2 < gidx), 1.0, 0.0))
    gkeep = grank < TOPK_GROUPS
    cur = jnp.concatenate(
        [jnp.where(gkeep[g:g + 1], groups[g], NEG_INF) for g in range(N_GROUPS)], axis=0)
    eidx = lax.broadcasted_iota(I32, (n_exp, C), 0).astype(F32)
    sels, idxs, ws = [], [], []
    for _ in range(MOE_TOPK):
        m = jnp.max(cur, axis=0, keepdims=True)
        ik = jnp.min(jnp.where(cur == m, eidx, float(n_exp)), axis=0, keepdims=True)
        sk = eidx == ik
        sels.append(sk)
        idxs.append(ik)
        ws.append(jnp.sum(jnp.where(sk, scores, 0.0), axis=0, keepdims=True))
        cur = jnp.where(sk, NEG_INF, cur)
    wsum = ws[0]
    for w in ws[1:]:
        wsum = wsum + w
    msel = jnp.zeros((n_exp, C), F32)
    for sk in sels:
        msel = msel + jnp.where(sk, 1.0, 0.0)
    msel_b = msel.astype(BF16)
    tr = lax.broadcasted_iota(I32, (C, C), 0)
    tc = lax.broadcasted_iota(I32, (C, C), 1)
    before = jnp.where(tr < tc, 1.0, 0.0).astype(BF16)
    prefix = carry + _dot(msel_b, before)
    ranks = [jnp.sum(jnp.where(sk, prefix, 0.0), axis=0, keepdims=True) for sk in sels]
    carry = carry + _dot(msel_b, jnp.ones((C, C), BF16))
    pad = [jnp.zeros((TOPK_PAD - MOE_TOPK, C), F32)]
    idx_rows = jnp.concatenate(idxs + pad, axis=0).astype(I32)
    w_rows = jnp.concatenate([w / wsum * ROUTED_SCALE for w in ws] + pad, axis=0)
    rank_rows = jnp.concatenate(ranks + pad, axis=0).astype(I32)
    return idx_rows, w_rows, rank_rows, carry


def _moe_pre_kernel(x_ref, nw_ref, sc_ref, sh_ref, wgu_ref, wd_ref, rwh_ref, rwl_ref, bias_ref,
                    h_ref, s_ref, idx_ref, w_ref, rank_ref, cnt_ref, carry_ref, *, n_exp, d_ff):
    @pl.when(pl.program_id(0) == 0)
    def _():
        carry_ref[...] = jnp.zeros_like(carry_ref)

    h = _prenorm(x_ref[...], nw_ref[...], sc_ref[0], sh_ref[0])
    h_hi, h_lo = _split_bf16(h)
    h_ref[...] = _pack_rows(h)
    gu = _dot(h_hi, wgu_ref[...])
    act = (_silu(gu[:, :d_ff]) * gu[:, d_ff:]).astype(BF16)
    s_ref[...] = _dot(act, wd_ref[...]).astype(s_ref.dtype)
    logits = _nt(rwh_ref[...], h_hi) + _nt(rwh_ref[...], h_lo) + _nt(rwl_ref[...], h_hi)
    carry = carry_ref[...]
    for c in range(x_ref.shape[0] // ROUTE_LANES):
        cols = pl.ds(c * ROUTE_LANES, ROUTE_LANES)
        i_rows, w_rows, r_rows, carry = _route_chunk(
            logits[:, c * ROUTE_LANES:(c + 1) * ROUTE_LANES], bias_ref[...], carry, n_exp)
        idx_ref[:, cols] = i_rows
        w_ref[:, cols] = w_rows
        rank_ref[:, cols] = r_rows
    carry_ref[...] = carry
    cnt_ref[...] = carry


def _moe_pre(x2, nw, mod3, sc_row, sh_row, wgu_bf, wd_bf, rw_hi, rw_lo, bias_col, seq):
    T, D = x2.shape
    n_exp = rw_hi.shape[0]
    d_ff = wd_bf.shape[0]
    tm = min(512, seq)
    per_b = seq // tm
    full = lambda a: pl.BlockSpec(a.shape, lambda i: (0,) * a.ndim)
    tok_rows = pl.BlockSpec((TOPK_PAD, tm), lambda i: (0, i))
    return pl.pallas_call(
        functools.partial(_moe_pre_kernel, n_exp=n_exp, d_ff=d_ff),
        grid=(T // tm,),
        in_specs=[pl.BlockSpec((tm, D), lambda i: (i, 0)),
                  pl.BlockSpec((1, D), lambda i: (0, 0)),
                  pl.BlockSpec((1, 1, D), lambda i: (sc_row(i // per_b), 0, 0)),
                  pl.BlockSpec((1, 1, D), lambda i: (sh_row(i // per_b), 0, 0)),
                  full(wgu_bf), full(wd_bf), full(rw_hi), full(rw_lo), full(bias_col)],
        out_specs=[pl.BlockSpec((tm, D // 2), lambda i: (i, 0)),
                   pl.BlockSpec((tm, D), lambda i: (i, 0)),
                   tok_rows, tok_rows, tok_rows,
                   pl.BlockSpec((n_exp, ROUTE_LANES), lambda i: (0, 0))],
        out_shape=[jax.ShapeDtypeStruct((T, D // 2), U32),
                   jax.ShapeDtypeStruct((T, D), BF16),
                   jax.ShapeDtypeStruct((TOPK_PAD, T), I32),
                   jax.ShapeDtypeStruct((TOPK_PAD, T), F32),
                   jax.ShapeDtypeStruct((TOPK_PAD, T), I32),
                   jax.ShapeDtypeStruct((n_exp, ROUTE_LANES), F32)],
        scratch_shapes=[pltpu.VMEM((n_exp, ROUTE_LANES), F32)],
        compiler_params=_cparams(("arbitrary",)),
        name="moe_pre",
    )(x2, nw, mod3, mod3, wgu_bf, wd_bf, rw_hi, rw_lo, bias_col)


def _dispatch_kernel(dest_ref, h_ref, xs_in, xs_out, sem):
    del xs_in
    tm = h_ref.shape[0]

    def row_copy(t, k):
        return pltpu.make_async_copy(h_ref.at[pl.ds(t, 1), :],
                                     xs_out.at[pl.ds(dest_ref[t * TOPK_PAD + k], 1), :], sem)

    def issue(t, c):
        for k in range(MOE_TOPK):
            row_copy(t, k).start()
        return c

    lax.fori_loop(0, tm, issue, 0)
    for _ in range(MOE_TOPK):
        pltpu.make_async_copy(h_ref, xs_out.at[pl.ds(0, tm), :], sem).wait()


def _dispatch(dest_flat, h32, n_slots):
    T, W = h32.shape
    tm = 256
    xs0 = jnp.zeros((n_slots, W), U32)
    return pl.pallas_call(
        _dispatch_kernel,
        grid=(T // tm,),
        in_specs=[pl.BlockSpec((tm * TOPK_PAD,), lambda i: (i,), memory_space=pltpu.SMEM),
                  pl.BlockSpec((tm, W), lambda i: (i, 0)),
                  pl.BlockSpec(memory_space=pl.ANY)],
        out_specs=pl.BlockSpec(memory_space=pl.ANY),
        out_shape=jax.ShapeDtypeStruct((n_slots, W), U32),
        scratch_shapes=[pltpu.SemaphoreType.DMA(())],
        input_output_aliases={2: 0},
        compiler_params=_cparams(("arbitrary",)),
        name="moe_dispatch",
    )(dest_flat, h32, xs0)


def _gmm_kernel(blk_e_ref, n_used_ref, xs_ref, wgu_ref, wd_ref, ys_ref, wgu_bf, wd_bf, *, d_ff):
    i = pl.program_id(0)
    used = i < n_used_ref[0]
    e_now = blk_e_ref[i]
    e_prev = blk_e_ref[jnp.maximum(i - 1, 0)]

    @pl.when(used & ((i == 0) | (e_now != e_prev)))
    def _():
        wgu_bf[...] = wgu_ref[0, 0].astype(BF16)
        wd_bf[...] = wd_ref[0, 0].astype(BF16)

    @pl.when(used)
    def _():
        x = _unpack_rows(xs_ref[...]).astype(BF16)
        gu = _dot(x, wgu_bf[...])
        act = (_silu(gu[:, :d_ff]) * gu[:, d_ff:]).astype(BF16)
        ys_ref[...] = _pack_rows(_dot(act, wd_bf[...]))

    @pl.when(jnp.logical_not(used))
    def _():
        ys_ref[...] = jnp.zeros_like(ys_ref)


def _gmm(blk_e, n_used, xs, w_gu, w_down, layer):
    P, W = xs.shape
    _, _, D, two_ff = w_gu.shape
    d_ff = two_ff // 2
    n_blocks = P // MOE_ROWS

    def live(i, be, nu):
        return jnp.minimum(i, nu[0] - 1)

    return pl.pallas_call(
        functools.partial(_gmm_kernel, d_ff=d_ff),
        grid_spec=pltpu.PrefetchScalarGridSpec(
            num_scalar_prefetch=2,
            grid=(n_blocks,),
            in_specs=[pl.BlockSpec((MOE_ROWS, W), lambda i, be, nu: (live(i, be, nu), 0)),
                      pl.BlockSpec((1, 1, D, two_ff), lambda i, be, nu: (layer, be[live(i, be, nu)], 0, 0)),
                      pl.BlockSpec((1, 1, d_ff, D), lambda i, be, nu: (layer, be[live(i, be, nu)], 0, 0))],
            out_specs=pl.BlockSpec((MOE_ROWS, W), lambda i, be, nu: (i, 0)),
            scratch_shapes=[pltpu.VMEM((D, two_ff), BF16), pltpu.VMEM((d_ff, D), BF16)]),
        out_shape=jax.ShapeDtypeStruct((P, W), U32),
        compiler_params=_cparams(("arbitrary",)),
        name="moe_gmm",
    )(blk_e, n_used, xs, w_gu, w_down)


def _combine_kernel(dest_ref, ys_hbm, w_ref, s_ref, x_ref, nw_ref, g_ref, o_ref, buf, sem):
    tm = x_ref.shape[0]

    def issue(t, c):
        for k in range(MOE_TOPK):
            pltpu.make_async_copy(ys_hbm.at[pl.ds(dest_ref[t * TOPK_PAD + k], 1), :],
                                  buf.at[pl.ds(k * tm + t, 1), :], sem).start()
        return c

    lax.fori_loop(0, tm, issue, 0)
    pltpu.make_async_copy(ys_hbm.at[pl.ds(0, MOE_TOPK * tm), :], buf, sem).wait()
    y = s_ref[...].astype(F32)
    w = w_ref[...]
    for k in range(MOE_TOPK):
        y = y + w[:, k:k + 1] * _unpack_rows(buf[pl.ds(k * tm, tm), :])
    o_ref[...] = x_ref[...] + g_ref[0] * (_rms(y) * nw_ref[...])


def _combine(dest_flat, ys, w_tok, shared, x2, nw, mod3, gate_row, seq):
    T, D = x2.shape
    W = ys.shape[1]
    tm = 256
    per_b = seq // tm
    return pl.pallas_call(
        _combine_kernel,
        grid=(T // tm,),
        in_specs=[pl.BlockSpec((tm * TOPK_PAD,), lambda i: (i,), memory_space=pltpu.SMEM),
                  pl.BlockSpec(memory_space=pl.ANY),
                  pl.BlockSpec((tm, TOPK_PAD), lambda i: (i, 0)),
                  pl.BlockSpec((tm, D), lambda i: (i, 0)),
                  pl.BlockSpec((tm, D), lambda i: (i, 0)),
                  pl.BlockSpec((1, D), lambda i: (0, 0)),
                  pl.BlockSpec((1, 1, D), lambda i: (gate_row(i // per_b), 0, 0))],
        out_specs=pl.BlockSpec((tm, D), lambda i: (i, 0)),
        out_shape=jax.ShapeDtypeStruct((T, D), F32),
        scratch_shapes=[pltpu.VMEM((MOE_TOPK * tm, W), U32), pltpu.SemaphoreType.DMA(())],
        compiler_params=_cparams(("arbitrary",)),
        name="moe_combine",
    )(dest_flat, ys, w_tok, shared, x2, nw, mod3)


def _moe_ffn(x2, nw_pre, nw_post, mod3, rows, router_w, router_bias, w_gu, w_down, sh_gu, sh_down, layer, seq):
    T, D = x2.shape
    n_exp = router_w.shape[1]
    rw_hi, rw_lo = _split_bf16(router_w.T)
    h32, shared, idx, w, rank, cnt = _moe_pre(
        x2, nw_pre, mod3, rows(4), rows(3), sh_gu.astype(BF16), sh_down.astype(BF16),
        rw_hi, rw_lo, router_bias.reshape(n_exp, 1), seq)
    counts = cnt[:, 0].astype(I32)
    padded = (counts + MOE_ROWS - 1) // MOE_ROWS * MOE_ROWS
    pad_end = jnp.cumsum(padded)
    pad_start = pad_end - padded
    n_blocks = -(-(T * MOE_TOPK + n_exp * (MOE_ROWS - 1)) // MOE_ROWS)
    dest = jnp.where(jnp.arange(TOPK_PAD)[:, None] < MOE_TOPK, pad_start[idx] + rank, 0)
    dest_flat = dest.T.reshape(T * TOPK_PAD).astype(I32)
    blk_e = jnp.minimum(jnp.searchsorted(pad_end, jnp.arange(n_blocks, dtype=I32) * MOE_ROWS, side="right"),
                        n_exp - 1).astype(I32)
    n_used = (pad_end[-1:] // MOE_ROWS).astype(I32)
    xs = _dispatch(dest_flat, h32, n_blocks * MOE_ROWS)
    ys = _gmm(blk_e, n_used, xs, w_gu, w_down, layer)
    return _combine(dest_flat, ys, w.T, shared, x2, nw_post, mod3, rows(5), seq)


def kernel(x, c, ada_w, ada_b, norm_w, gla_w_in, gla_w_gk_up, gla_b_gk, gla_norm_w, gla_w_out,
           moba_w_qkv, moba_w_out, router_w, router_bias, moe_w_gate_up, moe_w_down,
           shared_w_gate_up, shared_w_down):
    B, S, D = x.shape
    L = ada_w.shape[0]
    T = B * S
    key_dim = gla_w_gk_up.shape[2]
    val_dim = gla_w_out.shape[1]
    n_main = 2 * key_dim + 2 * val_dim

    mod3 = _ada_mod(c, ada_w, ada_b).reshape(L * B * 6, 1, D)
    cos, sin = _rope_tables(S)
    x2 = x.reshape(T, D)
    for i in range(L):
        rows = lambda part, i=i: (lambda b: (i * B + b) * 6 + part)
        nw = [norm_w[i, r].reshape(1, D) for r in range(4)]
        j = i // N_MIXERS
        if i % N_MIXERS == 0:
            w_gk = jnp.pad(gla_w_in[j][:, n_main:], ((0, 0), (0, LANES - GLA_GATE_RANK))).astype(BF16)
            proj, gk = _norm_proj(x2, nw[0], mod3, rows(1), rows(0), gla_w_in, j, n_main, S, w_gk=w_gk)
            wup_pad = jnp.pad(gla_w_gk_up[j], ((0, LANES - GLA_GATE_RANK), (0, 0)))
            a = _gla_core(proj, gk, wup_pad, gla_b_gk[j].reshape(1, key_dim),
                          gla_norm_w[j].reshape(1, -1), B, S, key_dim, val_dim)
            w_out = gla_w_out[j]
        else:
            (qkv,) = _norm_proj(x2, nw[0], mod3, rows(1), rows(0), moba_w_qkv, j, 3 * D, S)
            a = _moba_core(qkv, cos, sin, B, S, D)
            w_out = moba_w_out[j]
        x2 = _out_proj(a, w_out.astype(BF16), x2, nw[1], mod3, rows(2), S)
        x2 = _moe_ffn(x2, nw[2], nw[3], mod3, rows, router_w[i], router_bias[i], moe_w_gate_up, moe_w_down,
                      shared_w_gate_up[i], shared_w_down[i], i, S)
    return x2.reshape(B, S, D)
```

```python
import functools

import jax
import jax.numpy as jnp
from jax import lax
from jax.experimental import pallas as pl
from jax.experimental.pallas import tpu as pltpu

F32 = jnp.float32
BF16 = jnp.bfloat16
U32 = jnp.uint32
I32 = jnp.int32

NORM_EPS = 1e-6
N_MIXERS = 2
GLA_HEADS = 4
GLA_GATE_RANK = 16
GLA_GATE_NORMALIZER = 16.0
GLA_CHUNK = 64
MOBA_HEAD_DIM = 128
MOBA_BLOCK = 256
MOBA_TOPK = 3
ROPE_THETA = 10000.0
MOE_TOPK = 6
N_GROUPS = 8
TOPK_GROUPS = 4
ROUTED_SCALE = 2.5

LANES = 128
SUBLANES = 8
MOE_ROWS = 256
ROUTE_LANES = LANES
TOPK_PAD = 8
VMEM_LIMIT = 56 * 1024 * 1024

NEG_INF = float("-inf")
HI_MASK = 0xFFFF0000
LOG2E = 1.4426950408889634
ONES_ROWS = 16
ROW_TILE = SUBLANES


def _cparams(sem):
    return pltpu.CompilerParams(dimension_semantics=sem, vmem_limit_bytes=VMEM_LIMIT)


def _sigmoid(x):
    return 1.0 / (1.0 + jnp.exp(-x))


def _silu(x):
    return x * _sigmoid(x)


def _nt(a, b):
    return lax.dot_general(a, b, (((1,), (1,)), ((), ())), preferred_element_type=F32)


def _tn(a, b):
    return lax.dot_general(a, b, (((0,), (0,)), ((), ())), preferred_element_type=F32)


def _dot(a, b):
    return jnp.dot(a, b, preferred_element_type=F32)


def _split_bf16(x):
    hi = x.astype(BF16)
    lo = (x - hi.astype(F32)).astype(BF16)
    return hi, lo


def _pack_rows(y):
    w = y.shape[1] // 2
    bits = pltpu.bitcast(y.astype(BF16).astype(F32), U32)
    return (bits[:, :w] >> 16) | (bits[:, w:] & jnp.uint32(HI_MASK))


def _unpack_rows(w):
    lo = pltpu.bitcast(w << 16, F32)
    hi = pltpu.bitcast(w & jnp.uint32(HI_MASK), F32)
    return jnp.concatenate([lo, hi], axis=1)


def _store_rows(ref, first, packed):
    n, w = packed.shape
    rt = w // LANES
    for s in range(rt):
        ref[pl.ds(first * rt + s, n, stride=rt), :] = packed[:, s * LANES:(s + 1) * LANES]


def _load_rows(ref, first, n, rt):
    return jnp.concatenate([ref[pl.ds(first * rt + s, n, stride=rt), :] for s in range(rt)], axis=1)


def _rms(x):
    return x * lax.rsqrt(jnp.mean(x * x, axis=-1, keepdims=True) + NORM_EPS)


def _ada_kernel(c_ref, w_ref, b_ref, o_ref):
    cond = _silu(c_ref[...]).astype(BF16)
    o_ref[0] = _dot(cond, w_ref[0].astype(BF16)) + b_ref[0]


def _ada_mod(c, ada_w, ada_b):
    L, D, N = ada_w.shape
    B = c.shape[0]
    tn = 1024
    return pl.pallas_call(
        _ada_kernel,
        grid=(L, N // tn),
        in_specs=[pl.BlockSpec((B, D), lambda l, j: (0, 0)),
                  pl.BlockSpec((1, D, tn), lambda l, j: (l, 0, j)),
                  pl.BlockSpec((1, 1, tn), lambda l, j: (l, 0, j))],
        out_specs=pl.BlockSpec((1, B, tn), lambda l, j: (l, 0, j)),
        out_shape=jax.ShapeDtypeStruct((L, B, N), F32),
        compiler_params=_cparams(("arbitrary", "arbitrary")),
        name="ada_mod",
    )(c, ada_w, ada_b.reshape(L, 1, N))


def _prenorm(x, nw, scale, shift):
    return _rms(x) * nw * (1.0 + scale) + shift


def _normproj_kernel(x_ref, nw_ref, sc_ref, sh_ref, w_ref, *rest, with_gk):
    if with_gk:
        wg_ref, o_ref, og_ref, h_scr = rest
    else:
        o_ref, h_scr = rest

    @pl.when(pl.program_id(1) == 0)
    def _():
        h = _prenorm(x_ref[...], nw_ref[...], sc_ref[0], sh_ref[0])
        h_scr[...] = h.astype(BF16)
        if with_gk:
            og_ref[...] = _dot(h_scr[...], wg_ref[...].astype(BF16))

    o_ref[...] = _dot(h_scr[...], w_ref[0].astype(BF16)).astype(o_ref.dtype)


def _norm_proj(x2, nw, mod3, sc_row, sh_row, w3, widx, n_main, seq, w_gk=None):
    T, D = x2.shape
    tm, tn = min(1024, seq), 512
    per_b = seq // tm
    mod_spec = lambda row: pl.BlockSpec((1, 1, D), lambda i, j: (row(i // per_b), 0, 0))
    in_specs = [pl.BlockSpec((tm, D), lambda i, j: (i, 0)),
                pl.BlockSpec((1, D), lambda i, j: (0, 0)),
                mod_spec(sc_row), mod_spec(sh_row),
                pl.BlockSpec((1, D, tn), lambda i, j: (widx, 0, j))]
    out_specs = [pl.BlockSpec((tm, tn), lambda i, j: (i, j))]
    out_shape = [jax.ShapeDtypeStruct((T, n_main), BF16)]
    args = [x2, nw, mod3, mod3, w3]
    if w_gk is not None:
        in_specs.append(pl.BlockSpec((D, LANES), lambda i, j: (0, 0)))
        out_specs.append(pl.BlockSpec((tm, LANES), lambda i, j: (i, 0)))
        out_shape.append(jax.ShapeDtypeStruct((T, LANES), F32))
        args.append(w_gk)
    return pl.pallas_call(
        functools.partial(_normproj_kernel, with_gk=w_gk is not None),
        grid=(T // tm, n_main // tn),
        in_specs=in_specs, out_specs=out_specs, out_shape=out_shape,
        scratch_shapes=[pltpu.VMEM((tm, D), BF16)],
        compiler_params=_cparams(("arbitrary", "arbitrary")),
        name="norm_proj",
    )(*args)


def _outproj_kernel(a_ref, w_ref, x_ref, nw_ref, g_ref, o_ref):
    y = _dot(a_ref[...], w_ref[...])
    o_ref[...] = x_ref[...] + g_ref[0] * (_rms(y) * nw_ref[...])


def _out_proj(a, w_bf, x2, nw, mod3, gate_row, seq):
    T, D = x2.shape
    tm = min(512, seq)
    per_b = seq // tm
    return pl.pallas_call(
        _outproj_kernel,
        grid=(T // tm,),
        in_specs=[pl.BlockSpec((tm, a.shape[1]), lambda i: (i, 0)),
                  pl.BlockSpec(w_bf.shape, lambda i: (0, 0)),
                  pl.BlockSpec((tm, D), lambda i: (i, 0)),
                  pl.BlockSpec((1, D), lambda i: (0, 0)),
                  pl.BlockSpec((1, 1, D), lambda i: (gate_row(i // per_b), 0, 0))],
        out_specs=pl.BlockSpec((tm, D), lambda i: (i, 0)),
        out_shape=jax.ShapeDtypeStruct((T, D), F32),
        compiler_params=_cparams(("arbitrary",)),
        name="out_proj",
    )(a, w_bf, x2, nw, mod3)


def _gla_kernel(q_ref, k_ref, v_ref, g_ref, gk_ref, wup_ref, bgk_ref, nw_ref, o_ref, st_ref, *, n_chunk, dk):
    L = GLA_CHUNK

    @pl.when(pl.program_id(2) == 0)
    def _():
        st_ref[...] = jnp.zeros_like(st_ref)

    z = _dot(gk_ref[...], wup_ref[...]) + bgk_ref[...]
    log_a = (jnp.minimum(z, 0.0) - jnp.log(1.0 + jnp.exp(-jnp.abs(z)))) / GLA_GATE_NORMALIZER
    row = lax.broadcasted_iota(I32, (L, L), 0)
    col = lax.broadcasted_iota(I32, (L, L), 1)
    causal = row >= col
    tri = jnp.where(causal, 1.0, 0.0).astype(BF16)
    scale = dk ** -0.5
    for c in range(n_chunk):
        sl = pl.ds(c * L, L)
        la_hi, la_lo = _split_bf16(log_a[c * L:(c + 1) * L])
        b = _dot(tri, la_hi) + _dot(tri, la_lo)
        b_last = b[L - 1:L]
        q = q_ref[sl, :].astype(F32) * scale
        k = k_ref[sl, :].astype(F32)
        v = v_ref[sl, :]
        qe = (q * jnp.exp(b)).astype(BF16)
        ke = (k * jnp.exp(-b)).astype(BF16)
        kt = (k * jnp.exp(b_last - b)).astype(BF16)
        att = jnp.where(causal, _nt(qe, ke), 0.0)
        st = st_ref[...]
        o = _dot(att.astype(BF16), v) + _nt(qe, st.astype(BF16))
        st_ref[...] = st * jnp.exp(b_last) + _tn(v, kt)
        g = g_ref[sl, :].astype(F32)
        o_ref[sl, :] = (_rms(o) * nw_ref[...] * _silu(g)).astype(o_ref.dtype)


def _gla_core(proj, gk, wup_pad, bgk, gnw, batch, seq, key_dim, val_dim):
    T = proj.shape[0]
    H = GLA_HEADS
    dk, dv = key_dim // H, val_dim // H
    R = min(512, seq)
    nr = seq // R
    row = lambda b, h, r: b * nr + r
    return pl.pallas_call(
        functools.partial(_gla_kernel, n_chunk=R // GLA_CHUNK, dk=dk),
        grid=(batch, H, nr),
        in_specs=[pl.BlockSpec((R, dk), lambda b, h, r: (row(b, h, r), h)),
                  pl.BlockSpec((R, dk), lambda b, h, r: (row(b, h, r), H + h)),
                  pl.BlockSpec((R, dv), lambda b, h, r: (row(b, h, r), 2 * key_dim // dv + h)),
                  pl.BlockSpec((R, dv), lambda b, h, r: (row(b, h, r), (2 * key_dim + val_dim) // dv + h)),
                  pl.BlockSpec((R, LANES), lambda b, h, r: (row(b, h, r), 0)),
                  pl.BlockSpec((LANES, dk), lambda b, h, r: (0, h)),
                  pl.BlockSpec((1, dk), lambda b, h, r: (0, h)),
                  pl.BlockSpec((1, dv), lambda b, h, r: (0, 0))],
        out_specs=pl.BlockSpec((R, dv), lambda b, h, r: (row(b, h, r), h)),
        out_shape=jax.ShapeDtypeStruct((T, val_dim), BF16),
        scratch_shapes=[pltpu.VMEM((dv, dk), F32)],
        compiler_params=_cparams(("arbitrary", "arbitrary", "arbitrary")),
        name="gla_core",
    )(proj, proj, proj, proj, gk, wup_pad, bgk, gnw)


def _moba_kernel(q_ref, k_ref, v_ref, cos_ref, sin_ref, o_ref, s_scr, *, nb):
    blk, hd = MOBA_BLOCK, MOBA_HEAD_DIM
    S = nb * blk
    cos, sin = cos_ref[...], sin_ref[...]

    def rope(t_ref):
        t = t_ref[...].astype(F32)
        return t * cos + pltpu.roll(t, hd // 2, 1) * sin

    q = rope(q_ref) * (hd ** -0.5 * LOG2E)
    k = rope(k_ref)
    qb, kb = q.astype(BF16), k.astype(BF16)
    vT = jnp.concatenate([v_ref[...].astype(F32).T, jnp.ones((ONES_ROWS, S), F32)], axis=0).astype(BF16)

    n_sel = min(MOBA_TOPK, nb - 1)
    if n_sel > 0:
        kmean = jnp.concatenate(
            [jnp.mean(k[j * blk:(j + 1) * blk], axis=0, keepdims=True) for j in range(nb)], axis=0)
        jrow = lax.broadcasted_iota(I32, (nb, S), 0)
        qblk = lax.shift_right_logical(lax.broadcasted_iota(I32, (nb, S), 1), blk.bit_length() - 1)
        past = jrow < qblk
        gate = jnp.where(past, _nt(kmean, q), NEG_INF)
        rank = jnp.zeros((nb, S), F32)
        for j2 in range(nb):
            r = gate[j2:j2 + 1]
            beats = jnp.where(r > gate, 1.0, jnp.where((r == gate) & (j2 < jrow), 1.0, 0.0))
            rank = rank + beats
        sel = jnp.where(past & (rank < n_sel), 1.0, 0.0)

    kk = lax.broadcasted_iota(I32, (blk, blk), 0)
    qq = lax.broadcasted_iota(I32, (blk, blk), 1)
    for i in range(nb):
        qi = qb[i * blk:(i + 1) * blk]
        n_keys = (i + 1) if n_sel > 0 else 1
        s = jnp.where(kk <= qq, _nt(kb[i * blk:(i + 1) * blk], qi), NEG_INF)
        s_scr[pl.ds(0, blk), :] = s
        m = jnp.max(s, axis=0, keepdims=True)
        for j in range(n_keys - 1):
            keep = sel[j:j + 1, i * blk:(i + 1) * blk] > 0.0
            s = jnp.where(keep, _nt(kb[j * blk:(j + 1) * blk], qi), NEG_INF)
            s_scr[pl.ds((j + 1) * blk, blk), :] = s
            m = jnp.maximum(m, jnp.max(s, axis=0, keepdims=True))
        acc = jnp.zeros((hd + ONES_ROWS, blk), F32)
        for t in range(n_keys):
            j = i if t == 0 else t - 1
            p = jnp.exp2(s_scr[pl.ds(t * blk, blk), :] - m).astype(BF16)
            acc = acc + _dot(vT[:, j * blk:(j + 1) * blk], p)
        o_ref[pl.ds(i * blk, blk), :] = (acc[:hd] / acc[hd:hd + 1]).T.astype(o_ref.dtype)


def _moba_core(qkv, cos, sin, batch, seq, d_model):
    T = qkv.shape[0]
    hd = MOBA_HEAD_DIM
    H = d_model // hd
    nb = seq // MOBA_BLOCK
    return pl.pallas_call(
        functools.partial(_moba_kernel, nb=nb),
        grid=(batch, H),
        in_specs=[pl.BlockSpec((seq, hd), lambda b, h: (b, h)),
                  pl.BlockSpec((seq, hd), lambda b, h: (b, H + h)),
                  pl.BlockSpec((seq, hd), lambda b, h: (b, 2 * H + h)),
                  pl.BlockSpec((seq, hd), lambda b, h: (0, 0)),
                  pl.BlockSpec((seq, hd), lambda b, h: (0, 0))],
        out_specs=pl.BlockSpec((seq, hd), lambda b, h: (b, h)),
        out_shape=jax.ShapeDtypeStruct((T, d_model), BF16),
        scratch_shapes=[pltpu.VMEM((seq, MOBA_BLOCK), F32)],
        compiler_params=_cparams(("arbitrary", "arbitrary")),
        name="moba_core",
    )(qkv, qkv, qkv, cos, sin)


def _rope_tables(seq):
    hd = MOBA_HEAD_DIM
    half = hd // 2
    inv = jnp.power(ROPE_THETA, -jnp.arange(half, dtype=F32) * 2.0 / hd)
    ang = jnp.arange(seq).astype(F32)[:, None] * inv[None, :]
    cos, sin = jnp.cos(ang), jnp.sin(ang)
    return jnp.concatenate([cos, cos], axis=-1), jnp.concatenate([-sin, sin], axis=-1)


def _route_chunk(lt, bias, carry, n_exp):
    C = lt.shape[1]
    per_g = n_exp // N_GROUPS
    scores = _sigmoid(lt)
    biased = scores + bias
    sub = lax.broadcasted_iota(I32, (per_g, C), 0).astype(F32)
    gs_rows, groups = [], []
    for g in range(N_GROUPS):
        xg = biased[g * per_g:(g + 1) * per_g]
        groups.append(xg)
        m1 = jnp.max(xg, axis=0, keepdims=True)
        i1 = jnp.min(jnp.where(xg == m1, sub, float(per_g)), axis=0, keepdims=True)
        m2 = jnp.max(jnp.where(sub == i1, NEG_INF, xg), axis=0, keepdims=True)
        gs_rows.append(m1 + m2)
    gs = jnp.concatenate(gs_rows, axis=0)
    gidx = lax.broadcasted_iota(I32, (N_GROUPS, C), 0)
    grank = jnp.zeros((N_GROUPS, C), F32)
    for g2 in range(N_GROUPS):
        r = gs[g2:g2 + 1]
        grank = grank + jnp.where(r > gs, 1.0, jnp.where((r == gs) & (g2 < gidx), 1.0, 0.0))
    gkeep = grank < TOPK_GROUPS
    cur = jnp.concatenate(
        [jnp.where(gkeep[g:g + 1], groups[g], NEG_INF) for g in range(N_GROUPS)], axis=0)
    eidx = lax.broadcasted_iota(I32, (n_exp, C), 0).astype(F32)
    sels, idxs, ws = [], [], []
    for _ in range(MOE_TOPK):
        m = jnp.max(cur, axis=0, keepdims=True)
        ik = jnp.min(jnp.where(cur == m, eidx, float(n_exp)), axis=0, keepdims=True)
        sk = eidx == ik
        sels.append(sk)
        idxs.append(ik)
        ws.append(jnp.sum(jnp.where(sk, scores, 0.0), axis=0, keepdims=True))
        cur = jnp.where(sk, NEG_INF, cur)
    wsum = ws[0]
    for w in ws[1:]:
        wsum = wsum + w
    msel = jnp.zeros((n_exp, C), F32)
    for sk in sels:
        msel = msel + jnp.where(sk, 1.0, 0.0)
    msel_b = msel.astype(BF16)
    tr = lax.broadcasted_iota(I32, (C, C), 0)
    tc = lax.broadcasted_iota(I32, (C, C), 1)
    before = jnp.where(tr < tc, 1.0, 0.0).astype(BF16)
    prefix = carry + _dot(msel_b, before)
    ranks = [jnp.sum(jnp.where(sk, prefix, 0.0), axis=0, keepdims=True) for sk in sels]
    carry = carry + _dot(msel_b, jnp.ones((C, C), BF16))
    pad = [jnp.zeros((TOPK_PAD - MOE_TOPK, C), F32)]
    idx_rows = jnp.concatenate(idxs + pad, axis=0).astype(I32)
    w_rows = jnp.concatenate([w / wsum * ROUTED_SCALE for w in ws] + pad, axis=0)
    rank_rows = jnp.concatenate(ranks + pad, axis=0).astype(I32)
    return idx_rows, w_rows, rank_rows, carry


def _moe_pre_kernel(x_ref, nw_ref, sc_ref, sh_ref, wgu_ref, wd_ref, rwh_ref, rwl_ref, bias_ref,
                    h_ref, s_ref, idx_ref, w_ref, rank_ref, cnt_ref, carry_ref, *, n_exp, d_ff):
    @pl.when(pl.program_id(0) == 0)
    def _():
        carry_ref[...] = jnp.zeros_like(carry_ref)

    h = _prenorm(x_ref[...], nw_ref[...], sc_ref[0], sh_ref[0])
    h_hi, h_lo = _split_bf16(h)
    _store_rows(h_ref, 0, _pack_rows(h))
    gu = _dot(h_hi, wgu_ref[...])
    act = (_silu(gu[:, :d_ff]) * gu[:, d_ff:]).astype(BF16)
    s_ref[...] = _dot(act, wd_ref[...]).astype(s_ref.dtype)
    logits = _nt(rwh_ref[...], h_hi) + _nt(rwh_ref[...], h_lo) + _nt(rwl_ref[...], h_hi)
    carry = carry_ref[...]
    for c in range(x_ref.shape[0] // ROUTE_LANES):
        cols = pl.ds(c * ROUTE_LANES, ROUTE_LANES)
        i_rows, w_rows, r_rows, carry = _route_chunk(
            logits[:, c * ROUTE_LANES:(c + 1) * ROUTE_LANES], bias_ref[...], carry, n_exp)
        idx_ref[:, cols] = i_rows
        w_ref[:, cols] = w_rows
        rank_ref[:, cols] = r_rows
    carry_ref[...] = carry
    cnt_ref[...] = carry


def _moe_pre(x2, nw, mod3, sc_row, sh_row, wgu_bf, wd_bf, rw_hi, rw_lo, bias_col, seq):
    T, D = x2.shape
    n_exp = rw_hi.shape[0]
    d_ff = wd_bf.shape[0]
    rt = D // 2 // LANES
    tm = min(512, seq)
    per_b = seq // tm
    full = lambda a: pl.BlockSpec(a.shape, lambda i: (0,) * a.ndim)
    tok_rows = pl.BlockSpec((TOPK_PAD, tm), lambda i: (0, i))
    return pl.pallas_call(
        functools.partial(_moe_pre_kernel, n_exp=n_exp, d_ff=d_ff),
        grid=(T // tm,),
        in_specs=[pl.BlockSpec((tm, D), lambda i: (i, 0)),
                  pl.BlockSpec((1, D), lambda i: (0, 0)),
                  pl.BlockSpec((1, 1, D), lambda i: (sc_row(i // per_b), 0, 0)),
                  pl.BlockSpec((1, 1, D), lambda i: (sh_row(i // per_b), 0, 0)),
                  full(wgu_bf), full(wd_bf), full(rw_hi), full(rw_lo), full(bias_col)],
        out_specs=[pl.BlockSpec((tm * rt, LANES), lambda i: (i, 0)),
                   pl.BlockSpec((tm, D), lambda i: (i, 0)),
                   tok_rows, tok_rows, tok_rows,
                   pl.BlockSpec((n_exp, ROUTE_LANES), lambda i: (0, 0))],
        out_shape=[jax.ShapeDtypeStruct((T * rt, LANES), U32),
                   jax.ShapeDtypeStruct((T, D), BF16),
                   jax.ShapeDtypeStruct((TOPK_PAD, T), I32),
                   jax.ShapeDtypeStruct((TOPK_PAD, T), F32),
                   jax.ShapeDtypeStruct((TOPK_PAD, T), I32),
                   jax.ShapeDtypeStruct((n_exp, ROUTE_LANES), F32)],
        scratch_shapes=[pltpu.VMEM((n_exp, ROUTE_LANES), F32)],
        compiler_params=_cparams(("arbitrary",)),
        name="moe_pre",
    )(x2, nw, mod3, mod3, wgu_bf, wd_bf, rw_hi, rw_lo, bias_col)


def _tile_rows(ref, tile, rt):
    return ref.at[pl.ds(pl.multiple_of(tile * rt, rt), rt), :]


def _dispatch_kernel(dest_ref, h_ref, xs_in, xs_out, sem, *, rt):
    del xs_in
    tm = h_ref.shape[0] // rt

    def issue(t, c):
        for k in range(MOE_TOPK):
            pltpu.make_async_copy(_tile_rows(h_ref, t, rt),
                                  _tile_rows(xs_out, dest_ref[t * TOPK_PAD + k], rt), sem).start()
        return c

    lax.fori_loop(0, tm, issue, 0)
    for _ in range(MOE_TOPK):
        pltpu.make_async_copy(h_ref, xs_out.at[pl.ds(0, tm * rt), :], sem).wait()


def _dispatch(dest_flat, h32, n_tok, n_slots):
    rt = h32.shape[0] // n_tok
    tm = 256
    xs0 = jnp.zeros((n_slots * rt, LANES), U32)
    return pl.pallas_call(
        functools.partial(_dispatch_kernel, rt=rt),
        grid=(n_tok // tm,),
        in_specs=[pl.BlockSpec((tm * TOPK_PAD,), lambda i: (i,), memory_space=pltpu.SMEM),
                  pl.BlockSpec((tm * rt, LANES), lambda i: (i, 0)),
                  pl.BlockSpec(memory_space=pl.ANY)],
        out_specs=pl.BlockSpec(memory_space=pl.ANY),
        out_shape=jax.ShapeDtypeStruct((n_slots * rt, LANES), U32),
        scratch_shapes=[pltpu.SemaphoreType.DMA(())],
        input_output_aliases={2: 0},
        compiler_params=_cparams(("arbitrary",)),
        name="moe_dispatch",
    )(dest_flat, h32, xs0)


def _gmm_kernel(blk_e_ref, n_used_ref, xs_ref, wgu_ref, wd_ref, ys_ref, wgu_bf, wd_bf, act_scr, *, d_ff):
    i = pl.program_id(0)
    nu = n_used_ref[0]

    def expert(b):
        return blk_e_ref[jnp.clip(b, 0, nu - 1)]

    s1 = i < nu
    s2 = (i >= 1) & (i <= nu)

    @pl.when(s1 & ((i == 0) | (expert(i) != expert(i - 1))))
    def _():
        wgu_bf[...] = wgu_ref[0, 0].astype(BF16)

    @pl.when(s2 & ((i == 1) | (expert(i - 1) != expert(i - 2))))
    def _():
        wd_bf[...] = wd_ref[0, 0].astype(BF16)

    def stage1():
        rt = xs_ref.shape[0] // MOE_ROWS
        x = _unpack_rows(_load_rows(xs_ref, 0, MOE_ROWS, rt)).astype(BF16)
        gu = _dot(x, wgu_bf[...])
        act_scr[i % 2] = (_silu(gu[:, :d_ff]) * gu[:, d_ff:]).astype(BF16)

    def stage2():
        _store_rows(ys_ref, 0, _pack_rows(_dot(act_scr[(i + 1) % 2], wd_bf[...])))

    @pl.when(s1 & s2)
    def _():
        stage2()
        stage1()

    @pl.when(s1 & jnp.logical_not(s2))
    def _():
        stage1()

    @pl.when(s2 & jnp.logical_not(s1))
    def _():
        stage2()

    @pl.when(jnp.logical_not(s1 | s2))
    def _():
        ys_ref[...] = jnp.zeros_like(ys_ref)


def _gmm(blk_e, n_used, xs, w_gu, w_down, layer):
    _, _, D, two_ff = w_gu.shape
    d_ff = two_ff // 2
    rt = D // 2 // LANES
    n_blocks = xs.shape[0] // (MOE_ROWS * rt)

    def live(b, nu):
        return jnp.clip(b, 0, nu[0] - 1)

    return pl.pallas_call(
        functools.partial(_gmm_kernel, d_ff=d_ff),
        grid_spec=pltpu.PrefetchScalarGridSpec(
            num_scalar_prefetch=2,
            grid=(n_blocks + 1,),
            in_specs=[pl.BlockSpec((MOE_ROWS * rt, LANES), lambda i, be, nu: (live(i, nu), 0)),
                      pl.BlockSpec((1, 1, D, two_ff), lambda i, be, nu: (layer, be[live(i, nu)], 0, 0)),
                      pl.BlockSpec((1, 1, d_ff, D), lambda i, be, nu: (layer, be[live(i - 1, nu)], 0, 0))],
            out_specs=pl.BlockSpec((MOE_ROWS * rt, LANES), lambda i, be, nu: (jnp.maximum(i - 1, 0), 0)),
            scratch_shapes=[pltpu.VMEM((D, two_ff), BF16), pltpu.VMEM((d_ff, D), BF16),
                            pltpu.VMEM((2, MOE_ROWS, d_ff), BF16)]),
        out_shape=jax.ShapeDtypeStruct(xs.shape, U32),
        compiler_params=_cparams(("arbitrary",)),
        name="moe_gmm",
    )(blk_e, n_used, xs, w_gu, w_down)


def _combine_kernel(dest_ref, ys_hbm, w_ref, s_ref, x_ref, nw_ref, g_ref, o_ref, buf, sem, *, rt):
    tm = x_ref.shape[0]

    def issue(t, c):
        for k in range(MOE_TOPK):
            pltpu.make_async_copy(_tile_rows(ys_hbm, dest_ref[t * TOPK_PAD + k], rt),
                                  _tile_rows(buf, k * tm + t, rt), sem).start()
        return c

    lax.fori_loop(0, tm, issue, 0)
    pltpu.make_async_copy(ys_hbm.at[pl.ds(0, MOE_TOPK * tm * rt), :], buf, sem).wait()
    y = s_ref[...].astype(F32)
    w = w_ref[...]
    for k in range(MOE_TOPK):
        y = y + w[:, k:k + 1] * _unpack_rows(_load_rows(buf, k * tm, tm, rt))
    o_ref[...] = x_ref[...] + g_ref[0] * (_rms(y) * nw_ref[...])


def _combine(dest_flat, ys, w_tok, shared, x2, nw, mod3, gate_row, seq):
    T, D = x2.shape
    rt = D // 2 // LANES
    tm = 256
    per_b = seq // tm
    return pl.pallas_call(
        functools.partial(_combine_kernel, rt=rt),
        grid=(T // tm,),
        in_specs=[pl.BlockSpec((tm * TOPK_PAD,), lambda i: (i,), memory_space=pltpu.SMEM),
                  pl.BlockSpec(memory_space=pl.ANY),
                  pl.BlockSpec((tm, TOPK_PAD), lambda i: (i, 0)),
                  pl.BlockSpec((tm, D), lambda i: (i, 0)),
                  pl.BlockSpec((tm, D), lambda i: (i, 0)),
                  pl.BlockSpec((1, D), lambda i: (0, 0)),
                  pl.BlockSpec((1, 1, D), lambda i: (gate_row(i // per_b), 0, 0))],
        out_specs=pl.BlockSpec((tm, D), lambda i: (i, 0)),
        out_shape=jax.ShapeDtypeStruct((T, D), F32),
        scratch_shapes=[pltpu.VMEM((MOE_TOPK * tm * rt, LANES), U32), pltpu.SemaphoreType.DMA(())],
        compiler_params=_cparams(("arbitrary",)),
        name="moe_combine",
    )(dest_flat, ys, w_tok, shared, x2, nw, mod3)


def _moe_ffn(x2, nw_pre, nw_post, mod3, rows, router_w, router_bias, w_gu, w_down, sh_gu, sh_down, layer, seq):
    T, D = x2.shape
    n_exp = router_w.shape[1]
    rw_hi, rw_lo = _split_bf16(router_w.T)
    h32, shared, idx, w, rank, cnt = _moe_pre(
        x2, nw_pre, mod3, rows(4), rows(3), sh_gu.astype(BF16), sh_down.astype(BF16),
        rw_hi, rw_lo, router_bias.reshape(n_exp, 1), seq)
    counts = cnt[:, 0].astype(I32)
    padded = (counts + MOE_ROWS - 1) // MOE_ROWS * MOE_ROWS
    pad_end = jnp.cumsum(padded)
    pad_start = pad_end - padded
    n_blocks = -(-(T * MOE_TOPK + n_exp * (MOE_ROWS - 1)) // MOE_ROWS)
    start_of = jnp.sum(jnp.where(idx[:, :, None] == jnp.arange(n_exp, dtype=I32), pad_start, 0), axis=-1)
    dest = jnp.where(jnp.arange(TOPK_PAD)[:, None] < MOE_TOPK, start_of + rank, 0)
    dest_flat = dest.T.reshape(T * TOPK_PAD).astype(I32)
    blk_start = jnp.arange(n_blocks, dtype=I32) * MOE_ROWS
    blk_e = jnp.minimum(jnp.sum((pad_end[None, :] <= blk_start[:, None]).astype(I32), axis=1), n_exp - 1)
    n_used = (pad_end[-1:] // MOE_ROWS).astype(I32)
    xs = _dispatch(dest_flat, h32, T, n_blocks * MOE_ROWS)
    ys = _gmm(blk_e, n_used, xs, w_gu, w_down, layer)
    return _combine(dest_flat, ys, w.T, shared, x2, nw_post, mod3, rows(5), seq)


def kernel(x, c, ada_w, ada_b, norm_w, gla_w_in, gla_w_gk_up, gla_b_gk, gla_norm_w, gla_w_out,
           moba_w_qkv, moba_w_out, router_w, router_bias, moe_w_gate_up, moe_w_down,
           shared_w_gate_up, shared_w_down):
    B, S, D = x.shape
    L = ada_w.shape[0]
    T = B * S
    key_dim = gla_w_gk_up.shape[2]
    val_dim = gla_w_out.shape[1]
    n_main = 2 * key_dim + 2 * val_dim

    mod3 = _ada_mod(c, ada_w, ada_b).reshape(L * B * 6, 1, D)
    cos, sin = _rope_tables(S)
    x2 = x.reshape(T, D)
    for i in range(L):
        rows = lambda part, i=i: (lambda b: (i * B + b) * 6 + part)
        nw = [norm_w[i, r].reshape(1, D) for r in range(4)]
        j = i // N_MIXERS
        if i % N_MIXERS == 0:
            w_gk = jnp.pad(gla_w_in[j, :, n_main:], ((0, 0), (0, LANES - GLA_GATE_RANK)))
            proj, gk = _norm_proj(x2, nw[0], mod3, rows(1), rows(0), gla_w_in, j, n_main, S, w_gk=w_gk)
            wup_pad = jnp.pad(gla_w_gk_up[j], ((0, LANES - GLA_GATE_RANK), (0, 0)))
            a = _gla_core(proj, gk, wup_pad, gla_b_gk[j].reshape(1, key_dim),
                          gla_norm_w[j].reshape(1, -1), B, S, key_dim, val_dim)
            w_out = gla_w_out[j]
        else:
            (qkv,) = _norm_proj(x2, nw[0], mod3, rows(1), rows(0), moba_w_qkv, j, 3 * D, S)
            a = _moba_core(qkv, cos, sin, B, S, D)
            w_out = moba_w_out[j]
        x2 = _out_proj(a, w_out.astype(BF16), x2, nw[1], mod3, rows(2), S)
        x2 = _moe_ffn(x2, nw[2], nw[3], mod3, rows, router_w[i], router_bias[i], moe_w_gate_up, moe_w_down,
                      shared_w_gate_up[i], shared_w_down[i], i, S)
    return x2.reshape(B, S, D)
```

```python
import functools

import jax
import jax.numpy as jnp
from jax import lax
from jax.experimental import pallas as pl
from jax.experimental.pallas import tpu as pltpu

F32 = jnp.float32
BF16 = jnp.bfloat16
U32 = jnp.uint32
I32 = jnp.int32

NORM_EPS = 1e-6
N_MIXERS = 2
GLA_HEADS = 4
GLA_GATE_RANK = 16
GLA_GATE_NORMALIZER = 16.0
GLA_CHUNK = 64
MOBA_HEAD_DIM = 128
MOBA_BLOCK = 256
MOBA_TOPK = 3
ROPE_THETA = 10000.0
MOE_TOPK = 6
N_GROUPS = 8
TOPK_GROUPS = 4
ROUTED_SCALE = 2.5

LANES = 128
SUBLANES = 8
MOE_ROWS = 256
ROUTE_LANES = LANES
TOPK_PAD = 8
VMEM_LIMIT = 56 * 1024 * 1024

NEG_INF = float("-inf")
HI_MASK = 0xFFFF0000
LOG2E = 1.4426950408889634
ONES_ROWS = 16
ROW_TILE = SUBLANES


def _cparams(sem):
    return pltpu.CompilerParams(dimension_semantics=sem, vmem_limit_bytes=VMEM_LIMIT)


def _sigmoid(x):
    return 1.0 / (1.0 + jnp.exp(-x))


def _silu(x):
    return x * _sigmoid(x)


def _nt(a, b):
    return lax.dot_general(a, b, (((1,), (1,)), ((), ())), preferred_element_type=F32)


def _tn(a, b):
    return lax.dot_general(a, b, (((0,), (0,)), ((), ())), preferred_element_type=F32)


def _dot(a, b):
    return jnp.dot(a, b, preferred_element_type=F32)


def _split_bf16(x):
    hi = x.astype(BF16)
    lo = (x - hi.astype(F32)).astype(BF16)
    return hi, lo


def _pack_rows(y):
    w = y.shape[1] // 2
    bits = pltpu.bitcast(y.astype(BF16).astype(F32), U32)
    return (bits[:, :w] >> 16) | (bits[:, w:] & jnp.uint32(HI_MASK))


def _unpack_rows(w):
    lo = pltpu.bitcast(w << 16, F32)
    hi = pltpu.bitcast(w & jnp.uint32(HI_MASK), F32)
    return jnp.concatenate([lo, hi], axis=1)


def _store_rows(ref, first, packed):
    n, w = packed.shape
    rt = w // LANES
    for s in range(rt):
        ref[pl.ds(first * rt + s, n, stride=rt), :] = packed[:, s * LANES:(s + 1) * LANES]


def _load_rows(ref, first, n, rt):
    return jnp.concatenate([ref[pl.ds(first * rt + s, n, stride=rt), :] for s in range(rt)], axis=1)


def _rms(x):
    return x * lax.rsqrt(jnp.mean(x * x, axis=-1, keepdims=True) + NORM_EPS)


def _ada_kernel(c_ref, w_ref, b_ref, o_ref):
    cond = _silu(c_ref[...]).astype(BF16)
    o_ref[0] = _dot(cond, w_ref[0].astype(BF16)) + b_ref[0]


def _ada_mod(c, ada_w, ada_b):
    L, D, N = ada_w.shape
    B = c.shape[0]
    tn = 1024
    return pl.pallas_call(
        _ada_kernel,
        grid=(L, N // tn),
        in_specs=[pl.BlockSpec((B, D), lambda l, j: (0, 0)),
                  pl.BlockSpec((1, D, tn), lambda l, j: (l, 0, j)),
                  pl.BlockSpec((1, 1, tn), lambda l, j: (l, 0, j))],
        out_specs=pl.BlockSpec((1, B, tn), lambda l, j: (l, 0, j)),
        out_shape=jax.ShapeDtypeStruct((L, B, N), F32),
        compiler_params=_cparams(("arbitrary", "arbitrary")),
        name="ada_mod",
    )(c, ada_w, ada_b.reshape(L, 1, N))


def _prenorm(x, nw, scale, shift):
    return _rms(x) * nw * (1.0 + scale) + shift


def _normproj_kernel(x_ref, nw_ref, sc_ref, sh_ref, w_ref, *rest, with_gk):
    if with_gk:
        wg_ref, o_ref, og_ref, h_scr = rest
    else:
        o_ref, h_scr = rest

    @pl.when(pl.program_id(1) == 0)
    def _():
        h = _prenorm(x_ref[...], nw_ref[...], sc_ref[0], sh_ref[0])
        h_scr[...] = h.astype(BF16)
        if with_gk:
            og_ref[...] = _dot(h_scr[...], wg_ref[...].astype(BF16))

    o_ref[...] = _dot(h_scr[...], w_ref[0].astype(BF16)).astype(o_ref.dtype)


def _norm_proj(x2, nw, mod3, sc_row, sh_row, w3, widx, n_main, seq, w_gk=None):
    T, D = x2.shape
    tm = min(1024, seq)
    tn = 1024 if n_main % 1024 == 0 else 512
    per_b = seq // tm
    mod_spec = lambda row: pl.BlockSpec((1, 1, D), lambda i, j: (row(i // per_b), 0, 0))
    in_specs = [pl.BlockSpec((tm, D), lambda i, j: (i, 0)),
                pl.BlockSpec((1, D), lambda i, j: (0, 0)),
                mod_spec(sc_row), mod_spec(sh_row),
                pl.BlockSpec((1, D, tn), lambda i, j: (widx, 0, j))]
    out_specs = [pl.BlockSpec((tm, tn), lambda i, j: (i, j))]
    out_shape = [jax.ShapeDtypeStruct((T, n_main), BF16)]
    args = [x2, nw, mod3, mod3, w3]
    if w_gk is not None:
        in_specs.append(pl.BlockSpec((D, LANES), lambda i, j: (0, 0)))
        out_specs.append(pl.BlockSpec((tm, LANES), lambda i, j: (i, 0)))
        out_shape.append(jax.ShapeDtypeStruct((T, LANES), F32))
        args.append(w_gk)
    return pl.pallas_call(
        functools.partial(_normproj_kernel, with_gk=w_gk is not None),
        grid=(T // tm, n_main // tn),
        in_specs=in_specs, out_specs=out_specs, out_shape=out_shape,
        scratch_shapes=[pltpu.VMEM((tm, D), BF16)],
        compiler_params=_cparams(("arbitrary", "arbitrary")),
        name="norm_proj",
    )(*args)


def _outproj_kernel(a_ref, w_ref, x_ref, nw_ref, g_ref, o_ref):
    y = _dot(a_ref[...], w_ref[...])
    o_ref[...] = x_ref[...] + g_ref[0] * (_rms(y) * nw_ref[...])


def _out_proj(a, w_bf, x2, nw, mod3, gate_row, seq):
    T, D = x2.shape
    tm = min(512, seq)
    per_b = seq // tm
    return pl.pallas_call(
        _outproj_kernel,
        grid=(T // tm,),
        in_specs=[pl.BlockSpec((tm, a.shape[1]), lambda i: (i, 0)),
                  pl.BlockSpec(w_bf.shape, lambda i: (0, 0)),
                  pl.BlockSpec((tm, D), lambda i: (i, 0)),
                  pl.BlockSpec((1, D), lambda i: (0, 0)),
                  pl.BlockSpec((1, 1, D), lambda i: (gate_row(i // per_b), 0, 0))],
        out_specs=pl.BlockSpec((tm, D), lambda i: (i, 0)),
        out_shape=jax.ShapeDtypeStruct((T, D), F32),
        compiler_params=_cparams(("arbitrary",)),
        name="out_proj",
    )(a, w_bf, x2, nw, mod3)


def _gla_kernel(q_ref, k_ref, v_ref, g_ref, gk_ref, wup_ref, bgk_ref, nw_ref, o_ref, st_ref, *, n_chunk, dk):
    L = GLA_CHUNK

    @pl.when(pl.program_id(2) == 0)
    def _():
        st_ref[...] = jnp.zeros_like(st_ref)

    z = _dot(gk_ref[...], wup_ref[...]) + bgk_ref[...]
    log_a = (jnp.minimum(z, 0.0) - jnp.log(1.0 + jnp.exp(-jnp.abs(z)))) / GLA_GATE_NORMALIZER
    row = lax.broadcasted_iota(I32, (L, L), 0)
    col = lax.broadcasted_iota(I32, (L, L), 1)
    causal = row >= col
    tri = jnp.where(causal, 1.0, 0.0).astype(BF16)
    scale = dk ** -0.5
    for c in range(n_chunk):
        sl = pl.ds(c * L, L)
        la_hi, la_lo = _split_bf16(log_a[c * L:(c + 1) * L])
        b = _dot(tri, la_hi) + _dot(tri, la_lo)
        b_last = b[L - 1:L]
        q = q_ref[sl, :].astype(F32) * scale
        k = k_ref[sl, :].astype(F32)
        v = v_ref[sl, :]
        qe = (q * jnp.exp(b)).astype(BF16)
        ke = (k * jnp.exp(-b)).astype(BF16)
        kt = (k * jnp.exp(b_last - b)).astype(BF16)
        att = jnp.where(causal, _nt(qe, ke), 0.0)
        st = st_ref[...]
        o = _dot(att.astype(BF16), v) + _nt(qe, st.astype(BF16))
        st_ref[...] = st * jnp.exp(b_last) + _tn(v, kt)
        g = g_ref[sl, :].astype(F32)
        o_ref[sl, :] = (_rms(o) * nw_ref[...] * _silu(g)).astype(o_ref.dtype)


def _gla_core(proj, gk, wup_pad, bgk, gnw, batch, seq, key_dim, val_dim):
    T = proj.shape[0]
    H = GLA_HEADS
    dk, dv = key_dim // H, val_dim // H
    R = min(512, seq)
    nr = seq // R
    row = lambda b, h, r: b * nr + r
    return pl.pallas_call(
        functools.partial(_gla_kernel, n_chunk=R // GLA_CHUNK, dk=dk),
        grid=(batch, H, nr),
        in_specs=[pl.BlockSpec((R, dk), lambda b, h, r: (row(b, h, r), h)),
                  pl.BlockSpec((R, dk), lambda b, h, r: (row(b, h, r), H + h)),
                  pl.BlockSpec((R, dv), lambda b, h, r: (row(b, h, r), 2 * key_dim // dv + h)),
                  pl.BlockSpec((R, dv), lambda b, h, r: (row(b, h, r), (2 * key_dim + val_dim) // dv + h)),
                  pl.BlockSpec((R, LANES), lambda b, h, r: (row(b, h, r), 0)),
                  pl.BlockSpec((LANES, dk), lambda b, h, r: (0, h)),
                  pl.BlockSpec((1, dk), lambda b, h, r: (0, h)),
                  pl.BlockSpec((1, dv), lambda b, h, r: (0, 0))],
        out_specs=pl.BlockSpec((R, dv), lambda b, h, r: (row(b, h, r), h)),
        out_shape=jax.ShapeDtypeStruct((T, val_dim), BF16),
        scratch_shapes=[pltpu.VMEM((dv, dk), F32)],
        compiler_params=_cparams(("arbitrary", "arbitrary", "arbitrary")),
        name="gla_core",
    )(proj, proj, proj, proj, gk, wup_pad, bgk, gnw)


def _moba_kernel(q_ref, k_ref, v_ref, cos_ref, sin_ref, o_ref, s_scr, *, nb):
    blk, hd = MOBA_BLOCK, MOBA_HEAD_DIM
    S = nb * blk
    cos, sin = cos_ref[...], sin_ref[...]

    def rope(t_ref):
        t = t_ref[...].astype(F32)
        return t * cos + pltpu.roll(t, hd // 2, 1) * sin

    q = rope(q_ref) * (hd ** -0.5 * LOG2E)
    k = rope(k_ref)
    qb, kb = q.astype(BF16), k.astype(BF16)
    vT = jnp.concatenate([v_ref[...].astype(F32).T, jnp.ones((ONES_ROWS, S), F32)], axis=0).astype(BF16)

    n_sel = min(MOBA_TOPK, nb - 1)
    if n_sel > 0:
        kmean = jnp.concatenate(
            [jnp.mean(k[j * blk:(j + 1) * blk], axis=0, keepdims=True) for j in range(nb)], axis=0)
        jrow = lax.broadcasted_iota(I32, (nb, S), 0)
        qblk = lax.shift_right_logical(lax.broadcasted_iota(I32, (nb, S), 1), blk.bit_length() - 1)
        past = jrow < qblk
        gate = jnp.where(past, _nt(kmean, q), NEG_INF)
        rank = jnp.zeros((nb, S), F32)
        for j2 in range(nb):
            r = gate[j2:j2 + 1]
            beats = jnp.where(r > gate, 1.0, jnp.where((r == gate) & (j2 < jrow), 1.0, 0.0))
            rank = rank + beats
        sel = jnp.where(past & (rank < n_sel), 1.0, 0.0)

    kk = lax.broadcasted_iota(I32, (blk, blk), 0)
    qq = lax.broadcasted_iota(I32, (blk, blk), 1)
    for i in range(nb):
        qi = qb[i * blk:(i + 1) * blk]
        n_keys = (i + 1) if n_sel > 0 else 1
        s = jnp.where(kk <= qq, _nt(kb[i * blk:(i + 1) * blk], qi), NEG_INF)
        s_scr[pl.ds(0, blk), :] = s
        m = jnp.max(s, axis=0, keepdims=True)
        for j in range(n_keys - 1):
            keep = sel[j:j + 1, i * blk:(i + 1) * blk] > 0.0
            s = jnp.where(keep, _nt(kb[j * blk:(j + 1) * blk], qi), NEG_INF)
            s_scr[pl.ds((j + 1) * blk, blk), :] = s
            m = jnp.maximum(m, jnp.max(s, axis=0, keepdims=True))
        acc = jnp.zeros((hd + ONES_ROWS, blk), F32)
        for t in range(n_keys):
            j = i if t == 0 else t - 1
            p = jnp.exp2(s_scr[pl.ds(t * blk, blk), :] - m).astype(BF16)
            acc = acc + _dot(vT[:, j * blk:(j + 1) * blk], p)
        o_ref[pl.ds(i * blk, blk), :] = (acc[:hd] / acc[hd:hd + 1]).T.astype(o_ref.dtype)


def _moba_core(qkv, cos, sin, batch, seq, d_model):
    T = qkv.shape[0]
    hd = MOBA_HEAD_DIM
    H = d_model // hd
    nb = seq // MOBA_BLOCK
    return pl.pallas_call(
        functools.partial(_moba_kernel, nb=nb),
        grid=(batch, H),
        in_specs=[pl.BlockSpec((seq, hd), lambda b, h: (b, h)),
                  pl.BlockSpec((seq, hd), lambda b, h: (b, H + h)),
                  pl.BlockSpec((seq, hd), lambda b, h: (b, 2 * H + h)),
                  pl.BlockSpec((seq, hd), lambda b, h: (0, 0)),
                  pl.BlockSpec((seq, hd), lambda b, h: (0, 0))],
        out_specs=pl.BlockSpec((seq, hd), lambda b, h: (b, h)),
        out_shape=jax.ShapeDtypeStruct((T, d_model), BF16),
        scratch_shapes=[pltpu.VMEM((seq, MOBA_BLOCK), F32)],
        compiler_params=_cparams(("arbitrary", "arbitrary")),
        name="moba_core",
    )(qkv, qkv, qkv, cos, sin)


def _rope_tables(seq):
    hd = MOBA_HEAD_DIM
    half = hd // 2
    inv = jnp.power(ROPE_THETA, -jnp.arange(half, dtype=F32) * 2.0 / hd)
    ang = jnp.arange(seq).astype(F32)[:, None] * inv[None, :]
    cos, sin = jnp.cos(ang), jnp.sin(ang)
    return jnp.concatenate([cos, cos], axis=-1), jnp.concatenate([-sin, sin], axis=-1)


def _route_chunk(lt, bias, carry, n_exp):
    C = lt.shape[1]
    per_g = n_exp // N_GROUPS
    scores = _sigmoid(lt)
    biased = scores + bias
    sub = lax.broadcasted_iota(I32, (per_g, C), 0).astype(F32)
    gs_rows, groups = [], []
    for g in range(N_GROUPS):
        xg = biased[g * per_g:(g + 1) * per_g]
        groups.append(xg)
        m1 = jnp.max(xg, axis=0, keepdims=True)
        i1 = jnp.min(jnp.where(xg == m1, sub, float(per_g)), axis=0, keepdims=True)
        m2 = jnp.max(jnp.where(sub == i1, NEG_INF, xg), axis=0, keepdims=True)
        gs_rows.append(m1 + m2)
    gs = jnp.concatenate(gs_rows, axis=0)
    gidx = lax.broadcasted_iota(I32, (N_GROUPS, C), 0)
    grank = jnp.zeros((N_GROUPS, C), F32)
    for g2 in range(N_GROUPS):
        r = gs[g2:g2 + 1]
        grank = grank + jnp.where(r > gs, 1.0, jnp.where((r == gs) & (g2 < gidx), 1.0, 0.0))
    gkeep = grank < TOPK_GROUPS
    cur = jnp.concatenate(
        [jnp.where(gkeep[g:g + 1], groups[g], NEG_INF) for g in range(N_GROUPS)], axis=0)
    eidx = lax.broadcasted_iota(I32, (n_exp, C), 0).astype(F32)
    sels, idxs, ws = [], [], []
    for _ in range(MOE_TOPK):
        m = jnp.max(cur, axis=0, keepdims=True)
        ik = jnp.min(jnp.where(cur == m, eidx, float(n_exp)), axis=0, keepdims=True)
        sk = eidx == ik
        sels.append(sk)
        idxs.append(ik)
        ws.append(jnp.sum(jnp.where(sk, scores, 0.0), axis=0, keepdims=True))
        cur = jnp.where(sk, NEG_INF, cur)
    wsum = ws[0]
    for w in ws[1:]:
        wsum = wsum + w
    msel = jnp.zeros((n_exp, C), F32)
    for sk in sels:
        msel = msel + jnp.where(sk, 1.0, 0.0)
    msel_b = msel.astype(BF16)
    tr = lax.broadcasted_iota(I32, (C, C), 0)
    tc = lax.broadcasted_iota(I32, (C, C), 1)
    before = jnp.where(tr < tc, 1.0, 0.0).astype(BF16)
    prefix = carry + _dot(msel_b, before)
    ranks = [jnp.sum(jnp.where(sk, prefix, 0.0), axis=0, keepdims=True) for sk in sels]
    carry = carry + _dot(msel_b, jnp.ones((C, C), BF16))
    pad = [jnp.zeros((TOPK_PAD - MOE_TOPK, C), F32)]
    idx_rows = jnp.concatenate(idxs + pad, axis=0).astype(I32)
    w_rows = jnp.concatenate([w / wsum * ROUTED_SCALE for w in ws] + pad, axis=0)
    rank_rows = jnp.concatenate(ranks + pad, axis=0).astype(I32)
    return idx_rows, w_rows, rank_rows, carry


def _moe_pre_kernel(x_ref, nw_ref, sc_ref, sh_ref, wgu_ref, wd_ref, rwh_ref, rwl_ref, bias_ref,
                    h_ref, s_ref, idx_ref, w_ref, rank_ref, cnt_ref, carry_ref, *, n_exp, d_ff):
    @pl.when(pl.program_id(0) == 0)
    def _():
        carry_ref[...] = jnp.zeros_like(carry_ref)

    h = _prenorm(x_ref[...], nw_ref[...], sc_ref[0], sh_ref[0])
    h_hi, h_lo = _split_bf16(h)
    _store_rows(h_ref, 0, _pack_rows(h))
    gu = _dot(h_hi, wgu_ref[...])
    act = (_silu(gu[:, :d_ff]) * gu[:, d_ff:]).astype(BF16)
    s_ref[...] = _dot(act, wd_ref[...]).astype(s_ref.dtype)
    logits = _nt(rwh_ref[...], h_hi) + _nt(rwh_ref[...], h_lo) + _nt(rwl_ref[...], h_hi)
    carry = carry_ref[...]
    for c in range(x_ref.shape[0] // ROUTE_LANES):
        cols = pl.ds(c * ROUTE_LANES, ROUTE_LANES)
        i_rows, w_rows, r_rows, carry = _route_chunk(
            logits[:, c * ROUTE_LANES:(c + 1) * ROUTE_LANES], bias_ref[...], carry, n_exp)
        idx_ref[:, cols] = i_rows
        w_ref[:, cols] = w_rows
        rank_ref[:, cols] = r_rows
    carry_ref[...] = carry
    cnt_ref[...] = carry


def _moe_pre(x2, nw, mod3, sc_row, sh_row, wgu_bf, wd_bf, rw_hi, rw_lo, bias_col, seq):
    T, D = x2.shape
    n_exp = rw_hi.shape[0]
    d_ff = wd_bf.shape[0]
    rt = D // 2 // LANES
    tm = min(512, seq)
    per_b = seq // tm
    full = lambda a: pl.BlockSpec(a.shape, lambda i: (0,) * a.ndim)
    tok_rows = pl.BlockSpec((TOPK_PAD, tm), lambda i: (0, i))
    return pl.pallas_call(
        functools.partial(_moe_pre_kernel, n_exp=n_exp, d_ff=d_ff),
        grid=(T // tm,),
        in_specs=[pl.BlockSpec((tm, D), lambda i: (i, 0)),
                  pl.BlockSpec((1, D), lambda i: (0, 0)),
                  pl.BlockSpec((1, 1, D), lambda i: (sc_row(i // per_b), 0, 0)),
                  pl.BlockSpec((1, 1, D), lambda i: (sh_row(i // per_b), 0, 0)),
                  full(wgu_bf), full(wd_bf), full(rw_hi), full(rw_lo), full(bias_col)],
        out_specs=[pl.BlockSpec((tm * rt, LANES), lambda i: (i, 0)),
                   pl.BlockSpec((tm, D), lambda i: (i, 0)),
                   tok_rows, tok_rows, tok_rows,
                   pl.BlockSpec((n_exp, ROUTE_LANES), lambda i: (0, 0))],
        out_shape=[jax.ShapeDtypeStruct((T * rt, LANES), U32),
                   jax.ShapeDtypeStruct((T, D), BF16),
                   jax.ShapeDtypeStruct((TOPK_PAD, T), I32),
                   jax.ShapeDtypeStruct((TOPK_PAD, T), F32),
                   jax.ShapeDtypeStruct((TOPK_PAD, T), I32),
                   jax.ShapeDtypeStruct((n_exp, ROUTE_LANES), F32)],
        scratch_shapes=[pltpu.VMEM((n_exp, ROUTE_LANES), F32)],
        compiler_params=_cparams(("arbitrary",)),
        name="moe_pre",
    )(x2, nw, mod3, mod3, wgu_bf, wd_bf, rw_hi, rw_lo, bias_col)


def _tile_rows(ref, tile, rt):
    return ref.at[pl.ds(pl.multiple_of(tile * rt, rt), rt), :]


def _zero_fill(fill_start_ref, fill_n_ref, n_used_ref, zbuf, xs_out, zsem, *, rt, n_exp, n_blocks, wait):
    def copy(first_row, n_rows):
        cp = pltpu.make_async_copy(zbuf.at[pl.ds(0, n_rows * rt), :],
                                   xs_out.at[pl.ds(pl.multiple_of(first_row * rt, rt), n_rows * rt), :], zsem)
        cp.wait() if wait else cp.start()

    def per_expert(e, c):
        start, n = fill_start_ref[e], fill_n_ref[e]
        p = MOE_ROWS // 2
        while p >= 1:
            @pl.when((n & p) != 0)
            def _(p=p):
                copy(start + (n & ~(2 * p - 1)), p)
            p //= 2
        return c

    lax.fori_loop(0, n_exp, per_expert, 0)

    def per_tail_block(b, c):
        copy(b * MOE_ROWS, MOE_ROWS)
        return c

    lax.fori_loop(n_used_ref[0], n_blocks, per_tail_block, 0)


def _dispatch_kernel(fill_start_ref, fill_n_ref, n_used_ref, dest_ref, h_ref, xs_out, zbuf, sem, zsem,
                     *, rt, n_exp, n_blocks):
    tm = h_ref.shape[0] // rt
    fill = functools.partial(_zero_fill, fill_start_ref, fill_n_ref, n_used_ref, zbuf, xs_out, zsem,
                             rt=rt, n_exp=n_exp, n_blocks=n_blocks)

    @pl.when(pl.program_id(0) == 0)
    def _():
        zbuf[...] = jnp.zeros_like(zbuf)
        fill(wait=False)

    def issue(t, c):
        for k in range(MOE_TOPK):
            pltpu.make_async_copy(_tile_rows(h_ref, t, rt),
                                  _tile_rows(xs_out, dest_ref[t * TOPK_PAD + k], rt), sem).start(priority=k % 2)
        return c

    lax.fori_loop(0, tm, issue, 0)
    for _ in range(MOE_TOPK):
        pltpu.make_async_copy(h_ref, xs_out.at[pl.ds(0, tm * rt), :], sem).wait()

    @pl.when(pl.program_id(0) == 0)
    def _():
        fill(wait=True)


def _dispatch(dest_flat, h32, fill_start, fill_n, n_used, n_tok, n_blocks):
    rt = h32.shape[0] // n_tok
    tm = 256
    return pl.pallas_call(
        functools.partial(_dispatch_kernel, rt=rt, n_exp=fill_n.shape[0], n_blocks=n_blocks),
        grid_spec=pltpu.PrefetchScalarGridSpec(
            num_scalar_prefetch=3,
            grid=(n_tok // tm,),
            in_specs=[pl.BlockSpec((tm * TOPK_PAD,), lambda i, *_: (i,), memory_space=pltpu.SMEM),
                      pl.BlockSpec((tm * rt, LANES), lambda i, *_: (i, 0))],
            out_specs=pl.BlockSpec(memory_space=pl.ANY),
            scratch_shapes=[pltpu.VMEM((MOE_ROWS * rt, LANES), U32),
                            pltpu.SemaphoreType.DMA(()), pltpu.SemaphoreType.DMA(())]),
        out_shape=jax.ShapeDtypeStruct((n_blocks * MOE_ROWS * rt, LANES), U32),
        compiler_params=_cparams(("arbitrary",)),
        name="moe_dispatch",
    )(fill_start, fill_n, n_used, dest_flat, h32)


def _gmm_kernel(blk_e_ref, n_used_ref, xs_ref, wgu_ref, wd_ref, ys_ref, wgu_bf, wd_bf, act_scr, *, d_ff):
    i = pl.program_id(0)
    nu = n_used_ref[0]

    def expert(b):
        return blk_e_ref[jnp.clip(b, 0, nu - 1)]

    s1 = i < nu
    s2 = (i >= 1) & (i <= nu)

    @pl.when(s1 & ((i == 0) | (expert(i) != expert(i - 1))))
    def _():
        wgu_bf[...] = wgu_ref[0, 0].astype(BF16)

    @pl.when(s2 & ((i == 1) | (expert(i - 1) != expert(i - 2))))
    def _():
        wd_bf[...] = wd_ref[0, 0].astype(BF16)

    def stage1():
        rt = xs_ref.shape[0] // MOE_ROWS
        x = _unpack_rows(_load_rows(xs_ref, 0, MOE_ROWS, rt)).astype(BF16)
        gu = _dot(x, wgu_bf[...])
        act_scr[i % 2] = (_silu(gu[:, :d_ff]) * gu[:, d_ff:]).astype(BF16)

    def stage2():
        _store_rows(ys_ref, 0, _pack_rows(_dot(act_scr[(i + 1) % 2], wd_bf[...])))

    @pl.when(s1 & s2)
    def _():
        stage2()
        stage1()

    @pl.when(s1 & jnp.logical_not(s2))
    def _():
        stage1()

    @pl.when(s2 & jnp.logical_not(s1))
    def _():
        stage2()

    @pl.when(jnp.logical_not(s1 | s2))
    def _():
        ys_ref[...] = jnp.zeros_like(ys_ref)


def _gmm(blk_e, n_used, xs, w_gu, w_down, layer):
    _, _, D, two_ff = w_gu.shape
    d_ff = two_ff // 2
    rt = D // 2 // LANES
    n_blocks = xs.shape[0] // (MOE_ROWS * rt)

    def live(b, nu):
        return jnp.clip(b, 0, nu[0] - 1)

    return pl.pallas_call(
        functools.partial(_gmm_kernel, d_ff=d_ff),
        grid_spec=pltpu.PrefetchScalarGridSpec(
            num_scalar_prefetch=2,
            grid=(n_blocks + 1,),
            in_specs=[pl.BlockSpec((MOE_ROWS * rt, LANES), lambda i, be, nu: (live(i, nu), 0)),
                      pl.BlockSpec((1, 1, D, two_ff), lambda i, be, nu: (layer, be[live(i, nu)], 0, 0)),
                      pl.BlockSpec((1, 1, d_ff, D), lambda i, be, nu: (layer, be[live(i - 1, nu)], 0, 0))],
            out_specs=pl.BlockSpec((MOE_ROWS * rt, LANES), lambda i, be, nu: (jnp.maximum(i - 1, 0), 0)),
            scratch_shapes=[pltpu.VMEM((D, two_ff), BF16), pltpu.VMEM((d_ff, D), BF16),
                            pltpu.VMEM((2, MOE_ROWS, d_ff), BF16)]),
        out_shape=jax.ShapeDtypeStruct(xs.shape, U32),
        compiler_params=_cparams(("arbitrary",)),
        name="moe_gmm",
    )(blk_e, n_used, xs, w_gu, w_down)


def _combine_kernel(dest_ref, ys_hbm, w_ref, s_ref, x_ref, nw_ref, g_ref, o_ref, buf, sem, *, rt):
    tm = x_ref.shape[0]

    def issue(t, c):
        for k in range(MOE_TOPK):
            pltpu.make_async_copy(_tile_rows(ys_hbm, dest_ref[t * TOPK_PAD + k], rt),
                                  _tile_rows(buf, k * tm + t, rt), sem).start(priority=k % 2)
        return c

    lax.fori_loop(0, tm, issue, 0)
    pltpu.make_async_copy(ys_hbm.at[pl.ds(0, MOE_TOPK * tm * rt), :], buf, sem).wait()
    y = s_ref[...].astype(F32)
    w = w_ref[...]
    for k in range(MOE_TOPK):
        y = y + w[:, k:k + 1] * _unpack_rows(_load_rows(buf, k * tm, tm, rt))
    o_ref[...] = x_ref[...] + g_ref[0] * (_rms(y) * nw_ref[...])


def _combine(dest_flat, ys, w_tok, shared, x2, nw, mod3, gate_row, seq):
    T, D = x2.shape
    rt = D // 2 // LANES
    tm = 256
    per_b = seq // tm
    return pl.pallas_call(
        functools.partial(_combine_kernel, rt=rt),
        grid=(T // tm,),
        in_specs=[pl.BlockSpec((tm * TOPK_PAD,), lambda i: (i,), memory_space=pltpu.SMEM),
                  pl.BlockSpec(memory_space=pl.ANY),
                  pl.BlockSpec((tm, TOPK_PAD), lambda i: (i, 0)),
                  pl.BlockSpec((tm, D), lambda i: (i, 0)),
                  pl.BlockSpec((tm, D), lambda i: (i, 0)),
                  pl.BlockSpec((1, D), lambda i: (0, 0)),
                  pl.BlockSpec((1, 1, D), lambda i: (gate_row(i // per_b), 0, 0))],
        out_specs=pl.BlockSpec((tm, D), lambda i: (i, 0)),
        out_shape=jax.ShapeDtypeStruct((T, D), F32),
        scratch_shapes=[pltpu.VMEM((MOE_TOPK * tm * rt, LANES), U32), pltpu.SemaphoreType.DMA(())],
        compiler_params=_cparams(("arbitrary",)),
        name="moe_combine",
    )(dest_flat, ys, w_tok, shared, x2, nw, mod3)


def _moe_ffn(x2, nw_pre, nw_post, mod3, rows, router_w, router_bias, w_gu, w_down, sh_gu, sh_down, layer, seq):
    T, D = x2.shape
    n_exp = router_w.shape[1]
    rw_hi, rw_lo = _split_bf16(router_w.T)
    h32, shared, idx, w, rank, cnt = _moe_pre(
        x2, nw_pre, mod3, rows(4), rows(3), sh_gu.astype(BF16), sh_down.astype(BF16),
        rw_hi, rw_lo, router_bias.reshape(n_exp, 1), seq)
    counts = cnt[:, 0].astype(I32)
    padded = (counts + MOE_ROWS - 1) // MOE_ROWS * MOE_ROWS
    pad_end = jnp.cumsum(padded)
    pad_start = pad_end - padded
    n_blocks = -(-(T * MOE_TOPK + n_exp * (MOE_ROWS - 1)) // MOE_ROWS)
    start_of = jnp.sum(jnp.where(idx[:, :, None] == jnp.arange(n_exp, dtype=I32), pad_start, 0), axis=-1)
    dest = jnp.where(jnp.arange(TOPK_PAD)[:, None] < MOE_TOPK, start_of + rank, 0)
    dest_flat = dest.T.reshape(T * TOPK_PAD).astype(I32)
    blk_start = jnp.arange(n_blocks, dtype=I32) * MOE_ROWS
    blk_e = jnp.minimum(jnp.sum((pad_end[None, :] <= blk_start[:, None]).astype(I32), axis=1), n_exp - 1)
    n_used = (pad_end[-1:] // MOE_ROWS).astype(I32)
    xs = _dispatch(dest_flat, h32, (pad_start + counts).astype(I32), (padded - counts).astype(I32), n_used, T, n_blocks)
    ys = _gmm(blk_e, n_used, xs, w_gu, w_down, layer)
    return _combine(dest_flat, ys, w.T, shared, x2, nw_post, mod3, rows(5), seq)


def kernel(x, c, ada_w, ada_b, norm_w, gla_w_in, gla_w_gk_up, gla_b_gk, gla_norm_w, gla_w_out,
           moba_w_qkv, moba_w_out, router_w, router_bias, moe_w_gate_up, moe_w_down,
           shared_w_gate_up, shared_w_down):
    B, S, D = x.shape
    L = ada_w.shape[0]
    T = B * S
    key_dim = gla_w_gk_up.shape[2]
    val_dim = gla_w_out.shape[1]
    n_main = 2 * key_dim + 2 * val_dim

    mod3 = _ada_mod(c, ada_w, ada_b).reshape(L * B * 6, 1, D)
    cos, sin = _rope_tables(S)
    x2 = x.reshape(T, D)
    for i in range(L):
        rows = lambda part, i=i: (lambda b: (i * B + b) * 6 + part)
        nw = [norm_w[i, r].reshape(1, D) for r in range(4)]
        j = i // N_MIXERS
        if i % N_MIXERS == 0:
            w_gk = jnp.pad(gla_w_in[j, :, n_main:], ((0, 0), (0, LANES - GLA_GATE_RANK)))
            proj, gk = _norm_proj(x2, nw[0], mod3, rows(1), rows(0), gla_w_in, j, n_main, S, w_gk=w_gk)
            wup_pad = jnp.pad(gla_w_gk_up[j], ((0, LANES - GLA_GATE_RANK), (0, 0)))
            a = _gla_core(proj, gk, wup_pad, gla_b_gk[j].reshape(1, key_dim),
                          gla_norm_w[j].reshape(1, -1), B, S, key_dim, val_dim)
            w_out = gla_w_out[j]
        else:
            (qkv,) = _norm_proj(x2, nw[0], mod3, rows(1), rows(0), moba_w_qkv, j, 3 * D, S)
            a = _moba_core(qkv, cos, sin, B, S, D)
            w_out = moba_w_out[j]
        x2 = _out_proj(a, w_out.astype(BF16), x2, nw[1], mod3, rows(2), S)
        x2 = _moe_ffn(x2, nw[2], nw[3], mod3, rows, router_w[i], router_bias[i], moe_w_gate_up, moe_w_down,
                      shared_w_gate_up[i], shared_w_down[i], i, S)
    return x2.reshape(B, S, D)
```

```python
import functools

import jax
import jax.numpy as jnp
from jax import lax
from jax.experimental import pallas as pl
from jax.experimental.pallas import tpu as pltpu

F32 = jnp.float32
BF16 = jnp.bfloat16
U32 = jnp.uint32
I32 = jnp.int32

NORM_EPS = 1e-6
N_MIXERS = 2
GLA_HEADS = 4
GLA_GATE_RANK = 16
GLA_GATE_NORMALIZER = 16.0
GLA_CHUNK = 64
MOBA_HEAD_DIM = 128
MOBA_BLOCK = 256
MOBA_TOPK = 3
ROPE_THETA = 10000.0
MOE_TOPK = 6
N_GROUPS = 8
TOPK_GROUPS = 4
ROUTED_SCALE = 2.5

LANES = 128
SUBLANES = 8
MOE_ROWS = 256
ROUTE_LANES = LANES
TOPK_PAD = 8
VMEM_LIMIT = 56 * 1024 * 1024

NEG_INF = float("-inf")
HI_MASK = 0xFFFF0000
LOG2E = 1.4426950408889634
ONES_ROWS = 16
ROW_TILE = SUBLANES


def _cparams(sem):
    return pltpu.CompilerParams(dimension_semantics=sem, vmem_limit_bytes=VMEM_LIMIT)


def _sigmoid(x):
    return 1.0 / (1.0 + jnp.exp(-x))


def _silu(x):
    return x * _sigmoid(x)


def _nt(a, b):
    return lax.dot_general(a, b, (((1,), (1,)), ((), ())), preferred_element_type=F32)


def _tn(a, b):
    return lax.dot_general(a, b, (((0,), (0,)), ((), ())), preferred_element_type=F32)


def _dot(a, b):
    return jnp.dot(a, b, preferred_element_type=F32)


def _split_bf16(x):
    hi = x.astype(BF16)
    lo = (x - hi.astype(F32)).astype(BF16)
    return hi, lo


def _pack_rows(y):
    w = y.shape[1] // 2
    bits = pltpu.bitcast(y.astype(BF16).astype(F32), U32)
    return (bits[:, :w] >> 16) | (bits[:, w:] & jnp.uint32(HI_MASK))


def _unpack_rows(w):
    lo = pltpu.bitcast(w << 16, F32)
    hi = pltpu.bitcast(w & jnp.uint32(HI_MASK), F32)
    return jnp.concatenate([lo, hi], axis=1)


def _store_rows(ref, first, packed):
    n, w = packed.shape
    rt = w // LANES
    for s in range(rt):
        ref[pl.ds(first * rt + s, n, stride=rt), :] = packed[:, s * LANES:(s + 1) * LANES]


def _load_rows(ref, first, n, rt):
    return jnp.concatenate([ref[pl.ds(first * rt + s, n, stride=rt), :] for s in range(rt)], axis=1)


def _rms(x):
    return x * lax.rsqrt(jnp.mean(x * x, axis=-1, keepdims=True) + NORM_EPS)


def _ada_kernel(c_ref, w_ref, b_ref, o_ref):
    cond = _silu(c_ref[...]).astype(BF16)
    o_ref[0] = _dot(cond, w_ref[0].astype(BF16)) + b_ref[0]


def _ada_mod(c, ada_w, ada_b):
    L, D, N = ada_w.shape
    B = c.shape[0]
    tn = 1024
    return pl.pallas_call(
        _ada_kernel,
        grid=(L, N // tn),
        in_specs=[pl.BlockSpec((B, D), lambda l, j: (0, 0)),
                  pl.BlockSpec((1, D, tn), lambda l, j: (l, 0, j)),
                  pl.BlockSpec((1, 1, tn), lambda l, j: (l, 0, j))],
        out_specs=pl.BlockSpec((1, B, tn), lambda l, j: (l, 0, j)),
        out_shape=jax.ShapeDtypeStruct((L, B, N), F32),
        compiler_params=_cparams(("arbitrary", "arbitrary")),
        name="ada_mod",
    )(c, ada_w, ada_b.reshape(L, 1, N))


def _prenorm(x, nw, scale, shift):
    return _rms(x) * nw * (1.0 + scale) + shift


def _normproj_kernel(x_ref, nw_ref, sc_ref, sh_ref, w_ref, *rest, with_gk):
    if with_gk:
        wg_ref, o_ref, og_ref, h_scr = rest
    else:
        o_ref, h_scr = rest

    @pl.when(pl.program_id(1) == 0)
    def _():
        h = _prenorm(x_ref[...], nw_ref[...], sc_ref[0], sh_ref[0])
        h_scr[...] = h.astype(BF16)
        if with_gk:
            og_ref[...] = _dot(h_scr[...], wg_ref[...].astype(BF16))

    o_ref[...] = _dot(h_scr[...], w_ref[0].astype(BF16)).astype(o_ref.dtype)


def _norm_proj(x2, nw, mod3, sc_row, sh_row, w3, widx, n_main, seq, w_gk=None):
    T, D = x2.shape
    tm = min(1024, seq)
    tn = 1024 if n_main % 1024 == 0 else 512
    per_b = seq // tm
    mod_spec = lambda row: pl.BlockSpec((1, 1, D), lambda i, j: (row(i // per_b), 0, 0))
    in_specs = [pl.BlockSpec((tm, D), lambda i, j: (i, 0)),
                pl.BlockSpec((1, D), lambda i, j: (0, 0)),
                mod_spec(sc_row), mod_spec(sh_row),
                pl.BlockSpec((1, D, tn), lambda i, j: (widx, 0, j))]
    out_specs = [pl.BlockSpec((tm, tn), lambda i, j: (i, j))]
    out_shape = [jax.ShapeDtypeStruct((T, n_main), BF16)]
    args = [x2, nw, mod3, mod3, w3]
    if w_gk is not None:
        in_specs.append(pl.BlockSpec((D, LANES), lambda i, j: (0, 0)))
        out_specs.append(pl.BlockSpec((tm, LANES), lambda i, j: (i, 0)))
        out_shape.append(jax.ShapeDtypeStruct((T, LANES), F32))
        args.append(w_gk)
    return pl.pallas_call(
        functools.partial(_normproj_kernel, with_gk=w_gk is not None),
        grid=(T // tm, n_main // tn),
        in_specs=in_specs, out_specs=out_specs, out_shape=out_shape,
        scratch_shapes=[pltpu.VMEM((tm, D), BF16)],
        compiler_params=_cparams(("arbitrary", "arbitrary")),
        name="norm_proj",
    )(*args)


def _outproj_kernel(a_ref, w_ref, x_ref, nw_ref, g_ref, o_ref):
    y = _dot(a_ref[...], w_ref[...])
    o_ref[...] = x_ref[...] + g_ref[0] * (_rms(y) * nw_ref[...])


def _out_proj(a, w_bf, x2, nw, mod3, gate_row, seq):
    T, D = x2.shape
    tm = min(512, seq)
    per_b = seq // tm
    return pl.pallas_call(
        _outproj_kernel,
        grid=(T // tm,),
        in_specs=[pl.BlockSpec((tm, a.shape[1]), lambda i: (i, 0)),
                  pl.BlockSpec(w_bf.shape, lambda i: (0, 0)),
                  pl.BlockSpec((tm, D), lambda i: (i, 0)),
                  pl.BlockSpec((1, D), lambda i: (0, 0)),
                  pl.BlockSpec((1, 1, D), lambda i: (gate_row(i // per_b), 0, 0))],
        out_specs=pl.BlockSpec((tm, D), lambda i: (i, 0)),
        out_shape=jax.ShapeDtypeStruct((T, D), F32),
        compiler_params=_cparams(("arbitrary",)),
        name="out_proj",
    )(a, w_bf, x2, nw, mod3)


def _gla_kernel(q_ref, k_ref, v_ref, g_ref, gk_ref, wup_ref, bgk_ref, nw_ref, o_ref, st_ref, *, n_chunk, dk):
    L = GLA_CHUNK

    @pl.when(pl.program_id(2) == 0)
    def _():
        st_ref[...] = jnp.zeros_like(st_ref)

    z = _dot(gk_ref[...], wup_ref[...]) + bgk_ref[...]
    log_a = (jnp.minimum(z, 0.0) - jnp.log(1.0 + jnp.exp(-jnp.abs(z)))) / GLA_GATE_NORMALIZER
    row = lax.broadcasted_iota(I32, (L, L), 0)
    col = lax.broadcasted_iota(I32, (L, L), 1)
    causal = row >= col
    tri = jnp.where(causal, 1.0, 0.0).astype(BF16)
    scale = dk ** -0.5
    for c in range(n_chunk):
        sl = pl.ds(c * L, L)
        la_hi, la_lo = _split_bf16(log_a[c * L:(c + 1) * L])
        b = _dot(tri, la_hi) + _dot(tri, la_lo)
        b_last = b[L - 1:L]
        q = q_ref[sl, :].astype(F32) * scale
        k = k_ref[sl, :].astype(F32)
        v = v_ref[sl, :]
        qe = (q * jnp.exp(b)).astype(BF16)
        ke = (k * jnp.exp(-b)).astype(BF16)
        kt = (k * jnp.exp(b_last - b)).astype(BF16)
        att = jnp.where(causal, _nt(qe, ke), 0.0)
        st = st_ref[...]
        o = _dot(att.astype(BF16), v) + _nt(qe, st.astype(BF16))
        st_ref[...] = st * jnp.exp(b_last) + _tn(v, kt)
        g = g_ref[sl, :].astype(F32)
        o_ref[sl, :] = (_rms(o) * nw_ref[...] * _silu(g)).astype(o_ref.dtype)


def _gla_core(proj, gk, wup_pad, bgk, gnw, batch, seq, key_dim, val_dim):
    T = proj.shape[0]
    H = GLA_HEADS
    dk, dv = key_dim // H, val_dim // H
    R = min(512, seq)
    nr = seq // R
    row = lambda b, h, r: b * nr + r
    return pl.pallas_call(
        functools.partial(_gla_kernel, n_chunk=R // GLA_CHUNK, dk=dk),
        grid=(batch, H, nr),
        in_specs=[pl.BlockSpec((R, dk), lambda b, h, r: (row(b, h, r), h)),
                  pl.BlockSpec((R, dk), lambda b, h, r: (row(b, h, r), H + h)),
                  pl.BlockSpec((R, dv), lambda b, h, r: (row(b, h, r), 2 * key_dim // dv + h)),
                  pl.BlockSpec((R, dv), lambda b, h, r: (row(b, h, r), (2 * key_dim + val_dim) // dv + h)),
                  pl.BlockSpec((R, LANES), lambda b, h, r: (row(b, h, r), 0)),
                  pl.BlockSpec((LANES, dk), lambda b, h, r: (0, h)),
                  pl.BlockSpec((1, dk), lambda b, h, r: (0, h)),
                  pl.BlockSpec((1, dv), lambda b, h, r: (0, 0))],
        out_specs=pl.BlockSpec((R, dv), lambda b, h, r: (row(b, h, r), h)),
        out_shape=jax.ShapeDtypeStruct((T, val_dim), BF16),
        scratch_shapes=[pltpu.VMEM((dv, dk), F32)],
        compiler_params=_cparams(("arbitrary", "arbitrary", "arbitrary")),
        name="gla_core",
    )(proj, proj, proj, proj, gk, wup_pad, bgk, gnw)


def _moba_kernel(q_ref, k_ref, v_ref, cos_ref, sin_ref, o_ref, s_scr, *, nb):
    blk, hd = MOBA_BLOCK, MOBA_HEAD_DIM
    S = nb * blk
    cos, sin = cos_ref[...], sin_ref[...]

    def rope(t_ref):
        t = t_ref[...].astype(F32)
        return t * cos + pltpu.roll(t, hd // 2, 1) * sin

    q = rope(q_ref) * (hd ** -0.5 * LOG2E)
    k = rope(k_ref)
    qb, kb = q.astype(BF16), k.astype(BF16)
    vT = jnp.concatenate([v_ref[...].astype(F32).T, jnp.ones((ONES_ROWS, S), F32)], axis=0).astype(BF16)

    n_sel = min(MOBA_TOPK, nb - 1)
    if n_sel > 0:
        kmean = jnp.concatenate(
            [jnp.mean(k[j * blk:(j + 1) * blk], axis=0, keepdims=True) for j in range(nb)], axis=0)
        jrow = lax.broadcasted_iota(I32, (nb, S), 0)
        qblk = lax.shift_right_logical(lax.broadcasted_iota(I32, (nb, S), 1), blk.bit_length() - 1)
        past = jrow < qblk
        gate = jnp.where(past, _nt(kmean, q), NEG_INF)
        rank = jnp.zeros((nb, S), F32)
        for j2 in range(nb):
            r = gate[j2:j2 + 1]
            beats = jnp.where(r > gate, 1.0, jnp.where((r == gate) & (j2 < jrow), 1.0, 0.0))
            rank = rank + beats
        sel = jnp.where(past & (rank < n_sel), 1.0, 0.0)

    kk = lax.broadcasted_iota(I32, (blk, blk), 0)
    qq = lax.broadcasted_iota(I32, (blk, blk), 1)
    for i in range(nb):
        qi = qb[i * blk:(i + 1) * blk]
        n_keys = (i + 1) if n_sel > 0 else 1
        s = jnp.where(kk <= qq, _nt(kb[i * blk:(i + 1) * blk], qi), NEG_INF)
        s_scr[pl.ds(0, blk), :] = s
        m = jnp.max(s, axis=0, keepdims=True)
        for j in range(n_keys - 1):
            keep = sel[j:j + 1, i * blk:(i + 1) * blk] > 0.0
            s = jnp.where(keep, _nt(kb[j * blk:(j + 1) * blk], qi), NEG_INF)
            s_scr[pl.ds((j + 1) * blk, blk), :] = s
            m = jnp.maximum(m, jnp.max(s, axis=0, keepdims=True))
        acc = jnp.zeros((hd + ONES_ROWS, blk), F32)
        for t in range(n_keys):
            j = i if t == 0 else t - 1
            p = jnp.exp2(s_scr[pl.ds(t * blk, blk), :] - m).astype(BF16)
            acc = acc + _dot(vT[:, j * blk:(j + 1) * blk], p)
        o_ref[pl.ds(i * blk, blk), :] = (acc[:hd] / acc[hd:hd + 1]).T.astype(o_ref.dtype)


def _moba_core(qkv, cos, sin, batch, seq, d_model):
    T = qkv.shape[0]
    hd = MOBA_HEAD_DIM
    H = d_model // hd
    nb = seq // MOBA_BLOCK
    return pl.pallas_call(
        functools.partial(_moba_kernel, nb=nb),
        grid=(batch, H),
        in_specs=[pl.BlockSpec((seq, hd), lambda b, h: (b, h)),
                  pl.BlockSpec((seq, hd), lambda b, h: (b, H + h)),
                  pl.BlockSpec((seq, hd), lambda b, h: (b, 2 * H + h)),
                  pl.BlockSpec((seq, hd), lambda b, h: (0, 0)),
                  pl.BlockSpec((seq, hd), lambda b, h: (0, 0))],
        out_specs=pl.BlockSpec((seq, hd), lambda b, h: (b, h)),
        out_shape=jax.ShapeDtypeStruct((T, d_model), BF16),
        scratch_shapes=[pltpu.VMEM((seq, MOBA_BLOCK), F32)],
        compiler_params=_cparams(("arbitrary", "arbitrary")),
        name="moba_core",
    )(qkv, qkv, qkv, cos, sin)


def _rope_tables(seq):
    hd = MOBA_HEAD_DIM
    half = hd // 2
    inv = jnp.power(ROPE_THETA, -jnp.arange(half, dtype=F32) * 2.0 / hd)
    ang = jnp.arange(seq).astype(F32)[:, None] * inv[None, :]
    cos, sin = jnp.cos(ang), jnp.sin(ang)
    return jnp.concatenate([cos, cos], axis=-1), jnp.concatenate([-sin, sin], axis=-1)


def _route_chunk(lt, bias, carry, n_exp):
    C = lt.shape[1]
    per_g = n_exp // N_GROUPS
    scores = _sigmoid(lt)
    biased = scores + bias
    sub = lax.broadcasted_iota(I32, (per_g, C), 0).astype(F32)
    gs_rows, groups = [], []
    for g in range(N_GROUPS):
        xg = biased[g * per_g:(g + 1) * per_g]
        groups.append(xg)
        m1 = jnp.max(xg, axis=0, keepdims=True)
        i1 = jnp.min(jnp.where(xg == m1, sub, float(per_g)), axis=0, keepdims=True)
        m2 = jnp.max(jnp.where(sub == i1, NEG_INF, xg), axis=0, keepdims=True)
        gs_rows.append(m1 + m2)
    gs = jnp.concatenate(gs_rows, axis=0)
    gidx = lax.broadcasted_iota(I32, (N_GROUPS, C), 0)
    grank = jnp.zeros((N_GROUPS, C), F32)
    for g2 in range(N_GROUPS):
        r = gs[g2:g2 + 1]
        grank = grank + jnp.where(r > gs, 1.0, jnp.where((r == gs) & (g2 < gidx), 1.0, 0.0))
    gkeep = grank < TOPK_GROUPS
    cur = jnp.concatenate(
        [jnp.where(gkeep[g:g + 1], groups[g], NEG_INF) for g in range(N_GROUPS)], axis=0)
    eidx = lax.broadcasted_iota(I32, (n_exp, C), 0).astype(F32)
    sels, idxs, ws = [], [], []
    for _ in range(MOE_TOPK):
        m = jnp.max(cur, axis=0, keepdims=True)
        ik = jnp.min(jnp.where(cur == m, eidx, float(n_exp)), axis=0, keepdims=True)
        sk = eidx == ik
        sels.append(sk)
        idxs.append(ik)
        ws.append(jnp.sum(jnp.where(sk, scores, 0.0), axis=0, keepdims=True))
        cur = jnp.where(sk, NEG_INF, cur)
    wsum = ws[0]
    for w in ws[1:]:
        wsum = wsum + w
    msel = jnp.zeros((n_exp, C), F32)
    for sk in sels:
        msel = msel + jnp.where(sk, 1.0, 0.0)
    msel_b = msel.astype(BF16)
    tr = lax.broadcasted_iota(I32, (C, C), 0)
    tc = lax.broadcasted_iota(I32, (C, C), 1)
    before = jnp.where(tr < tc, 1.0, 0.0).astype(BF16)
    prefix = carry + _dot(msel_b, before)
    ranks = [jnp.sum(jnp.where(sk, prefix, 0.0), axis=0, keepdims=True) for sk in sels]
    carry = carry + _dot(msel_b, jnp.ones((C, C), BF16))
    pad = [jnp.zeros((TOPK_PAD - MOE_TOPK, C), F32)]
    idx_rows = jnp.concatenate(idxs + pad, axis=0).astype(I32)
    w_rows = jnp.concatenate([w / wsum * ROUTED_SCALE for w in ws] + pad, axis=0)
    rank_rows = jnp.concatenate(ranks + pad, axis=0).astype(I32)
    return idx_rows, w_rows, rank_rows, carry


def _moe_pre_kernel(x_ref, nw_ref, sc_ref, sh_ref, wgu_ref, wd_ref, rwh_ref, rwl_ref, bias_ref,
                    h_ref, s_ref, idx_ref, w_ref, rank_ref, cnt_ref, carry_ref, *, n_exp, d_ff):
    @pl.when(pl.program_id(0) == 0)
    def _():
        carry_ref[...] = jnp.zeros_like(carry_ref)

    h = _prenorm(x_ref[...], nw_ref[...], sc_ref[0], sh_ref[0])
    h_hi, h_lo = _split_bf16(h)
    _store_rows(h_ref, 0, _pack_rows(h))
    gu = _dot(h_hi, wgu_ref[...])
    act = (_silu(gu[:, :d_ff]) * gu[:, d_ff:]).astype(BF16)
    s_ref[...] = _dot(act, wd_ref[...]).astype(s_ref.dtype)
    logits = _nt(rwh_ref[...], h_hi) + _nt(rwh_ref[...], h_lo) + _nt(rwl_ref[...], h_hi)
    carry = carry_ref[...]
    for c in range(x_ref.shape[0] // ROUTE_LANES):
        cols = pl.ds(c * ROUTE_LANES, ROUTE_LANES)
        i_rows, w_rows, r_rows, carry = _route_chunk(
            logits[:, c * ROUTE_LANES:(c + 1) * ROUTE_LANES], bias_ref[...], carry, n_exp)
        idx_ref[:, cols] = i_rows
        w_ref[:, cols] = w_rows
        rank_ref[:, cols] = r_rows
    carry_ref[...] = carry
    cnt_ref[...] = carry


def _moe_pre(x2, nw, mod3, sc_row, sh_row, wgu_bf, wd_bf, rw_hi, rw_lo, bias_col, seq):
    T, D = x2.shape
    n_exp = rw_hi.shape[0]
    d_ff = wd_bf.shape[0]
    rt = D // 2 // LANES
    tm = min(512, seq)
    per_b = seq // tm
    full = lambda a: pl.BlockSpec(a.shape, lambda i: (0,) * a.ndim)
    tok_rows = pl.BlockSpec((TOPK_PAD, tm), lambda i: (0, i))
    return pl.pallas_call(
        functools.partial(_moe_pre_kernel, n_exp=n_exp, d_ff=d_ff),
        grid=(T // tm,),
        in_specs=[pl.BlockSpec((tm, D), lambda i: (i, 0)),
                  pl.BlockSpec((1, D), lambda i: (0, 0)),
                  pl.BlockSpec((1, 1, D), lambda i: (sc_row(i // per_b), 0, 0)),
                  pl.BlockSpec((1, 1, D), lambda i: (sh_row(i // per_b), 0, 0)),
                  full(wgu_bf), full(wd_bf), full(rw_hi), full(rw_lo), full(bias_col)],
        out_specs=[pl.BlockSpec((tm * rt, LANES), lambda i: (i, 0)),
                   pl.BlockSpec((tm, D), lambda i: (i, 0)),
                   tok_rows, tok_rows, tok_rows,
                   pl.BlockSpec((n_exp, ROUTE_LANES), lambda i: (0, 0))],
        out_shape=[jax.ShapeDtypeStruct((T * rt, LANES), U32),
                   jax.ShapeDtypeStruct((T, D), BF16),
                   jax.ShapeDtypeStruct((TOPK_PAD, T), I32),
                   jax.ShapeDtypeStruct((TOPK_PAD, T), F32),
                   jax.ShapeDtypeStruct((TOPK_PAD, T), I32),
                   jax.ShapeDtypeStruct((n_exp, ROUTE_LANES), F32)],
        scratch_shapes=[pltpu.VMEM((n_exp, ROUTE_LANES), F32)],
        compiler_params=_cparams(("arbitrary",)),
        name="moe_pre",
    )(x2, nw, mod3, mod3, wgu_bf, wd_bf, rw_hi, rw_lo, bias_col)


def _tile_rows(ref, tile, rt):
    return ref.at[pl.ds(pl.multiple_of(tile * rt, rt), rt), :]


def _zero_fill(fill_start_ref, fill_n_ref, n_used_ref, zbuf, xs_out, zsem, *, rt, n_exp, n_blocks, wait):
    def copy(first_row, n_rows):
        cp = pltpu.make_async_copy(zbuf.at[pl.ds(0, n_rows * rt), :],
                                   xs_out.at[pl.ds(pl.multiple_of(first_row * rt, rt), n_rows * rt), :], zsem)
        cp.wait() if wait else cp.start()

    def per_expert(e, c):
        start, n = fill_start_ref[e], fill_n_ref[e]
        p = MOE_ROWS // 2
        while p >= 1:
            @pl.when((n & p) != 0)
            def _(p=p):
                copy(start + (n & ~(2 * p - 1)), p)
            p //= 2
        return c

    lax.fori_loop(0, n_exp, per_expert, 0)

    def per_tail_block(b, c):
        copy(b * MOE_ROWS, MOE_ROWS)
        return c

    lax.fori_loop(n_used_ref[0], n_blocks, per_tail_block, 0)


def _dispatch_kernel(fill_start_ref, fill_n_ref, n_used_ref, dest_ref, h_ref, xs_out, zbuf, sem, zsem,
                     *, rt, n_exp, n_blocks):
    tm = h_ref.shape[0] // rt
    fill = functools.partial(_zero_fill, fill_start_ref, fill_n_ref, n_used_ref, zbuf, xs_out, zsem,
                             rt=rt, n_exp=n_exp, n_blocks=n_blocks)

    @pl.when(pl.program_id(0) == 0)
    def _():
        zbuf[...] = jnp.zeros_like(zbuf)
        fill(wait=False)

    def issue(t, c):
        for k in range(MOE_TOPK):
            pltpu.make_async_copy(_tile_rows(h_ref, t, rt),
                                  _tile_rows(xs_out, dest_ref[t * TOPK_PAD + k], rt), sem).start(priority=k % 2)
        return c

    lax.fori_loop(0, tm, issue, 0)
    for _ in range(MOE_TOPK):
        pltpu.make_async_copy(h_ref, xs_out.at[pl.ds(0, tm * rt), :], sem).wait()

    @pl.when(pl.program_id(0) == 0)
    def _():
        fill(wait=True)


def _dispatch(dest_flat, h32, fill_start, fill_n, n_used, n_tok, n_blocks):
    rt = h32.shape[0] // n_tok
    tm = 256
    return pl.pallas_call(
        functools.partial(_dispatch_kernel, rt=rt, n_exp=fill_n.shape[0], n_blocks=n_blocks),
        grid_spec=pltpu.PrefetchScalarGridSpec(
            num_scalar_prefetch=3,
            grid=(n_tok // tm,),
            in_specs=[pl.BlockSpec((tm * TOPK_PAD,), lambda i, *_: (i,), memory_space=pltpu.SMEM),
                      pl.BlockSpec((tm * rt, LANES), lambda i, *_: (i, 0))],
            out_specs=pl.BlockSpec(memory_space=pl.ANY),
            scratch_shapes=[pltpu.VMEM((MOE_ROWS * rt, LANES), U32),
                            pltpu.SemaphoreType.DMA(()), pltpu.SemaphoreType.DMA(())]),
        out_shape=jax.ShapeDtypeStruct((n_blocks * MOE_ROWS * rt, LANES), U32),
        compiler_params=_cparams(("arbitrary",)),
        name="moe_dispatch",
    )(fill_start, fill_n, n_used, dest_flat, h32)


def _gmm_kernel(blk_e_ref, n_used_ref, nxt_e_ref, xs_ref, wgu_ref, wd_ref, ys_ref, wgu_bf, wd_bf, act_scr, *, d_ff):
    del nxt_e_ref
    i = pl.program_id(0)
    nu = n_used_ref[0]

    def expert(b):
        return blk_e_ref[jnp.clip(b, 0, nu - 1)]

    s1 = i < nu
    s2 = (i >= 1) & (i <= nu)

    @pl.when(s1 & ((i == 0) | (expert(i) != expert(i - 1))))
    def _():
        wgu_bf[...] = wgu_ref[0, 0].astype(BF16)

    @pl.when(s2 & ((i == 1) | (expert(i - 1) != expert(i - 2))))
    def _():
        wd_bf[...] = wd_ref[0, 0].astype(BF16)

    def stage1():
        rt = xs_ref.shape[0] // MOE_ROWS
        x = _unpack_rows(_load_rows(xs_ref, 0, MOE_ROWS, rt)).astype(BF16)
        gu = _dot(x, wgu_bf[...])
        act_scr[i % 2] = (_silu(gu[:, :d_ff]) * gu[:, d_ff:]).astype(BF16)

    def stage2():
        _store_rows(ys_ref, 0, _pack_rows(_dot(act_scr[(i + 1) % 2], wd_bf[...])))

    @pl.when(s1 & s2)
    def _():
        stage2()
        stage1()

    @pl.when(s1 & jnp.logical_not(s2))
    def _():
        stage1()

    @pl.when(s2 & jnp.logical_not(s1))
    def _():
        stage2()

    @pl.when(jnp.logical_not(s1 | s2))
    def _():
        ys_ref[...] = jnp.zeros_like(ys_ref)


def _gmm(blk_e, n_used, nxt_e, xs, w_gu, w_down, layer):
    _, _, D, two_ff = w_gu.shape
    d_ff = two_ff // 2
    rt = D // 2 // LANES
    n_blocks = xs.shape[0] // (MOE_ROWS * rt)

    def live(b, nu):
        return jnp.clip(b, 0, nu[0] - 1)

    def ahead(b, be, nu, nx):
        return jnp.where(b < 0, be[0], nx[live(b, nu)])

    return pl.pallas_call(
        functools.partial(_gmm_kernel, d_ff=d_ff),
        grid_spec=pltpu.PrefetchScalarGridSpec(
            num_scalar_prefetch=3,
            grid=(n_blocks + 1,),
            in_specs=[pl.BlockSpec((MOE_ROWS * rt, LANES), lambda i, be, nu, nx: (live(i, nu), 0)),
                      pl.BlockSpec((1, 1, D, two_ff), lambda i, be, nu, nx: (layer, ahead(i - 1, be, nu, nx), 0, 0)),
                      pl.BlockSpec((1, 1, d_ff, D), lambda i, be, nu, nx: (layer, ahead(i - 2, be, nu, nx), 0, 0))],
            out_specs=pl.BlockSpec((MOE_ROWS * rt, LANES), lambda i, be, nu, nx: (jnp.maximum(i - 1, 0), 0)),
            scratch_shapes=[pltpu.VMEM((D, two_ff), BF16), pltpu.VMEM((d_ff, D), BF16),
                            pltpu.VMEM((2, MOE_ROWS, d_ff), BF16)]),
        out_shape=jax.ShapeDtypeStruct(xs.shape, U32),
        compiler_params=_cparams(("arbitrary",)),
        name="moe_gmm",
    )(blk_e, n_used, nxt_e, xs, w_gu, w_down)


def _combine_kernel(dest_ref, dest_next_ref, ys_hbm, w_ref, s_ref, x_ref, nw_ref, g_ref, o_ref, buf, sem, *, rt):
    tm = x_ref.shape[0]
    i = pl.program_id(0)
    half = MOE_TOPK * tm
    slot = i % 2

    def gather(d_ref, into):
        def issue(t, c):
            for k in range(MOE_TOPK):
                pltpu.make_async_copy(_tile_rows(ys_hbm, d_ref[t * TOPK_PAD + k], rt),
                                      _tile_rows(buf, into * half + k * tm + t, rt),
                                      sem.at[into]).start(priority=k % 2)
            return c
        lax.fori_loop(0, tm, issue, 0)

    @pl.when(i == 0)
    def _():
        gather(dest_ref, 0)

    @pl.when(i + 1 < pl.num_programs(0))
    def _():
        gather(dest_next_ref, 1 - slot)

    pltpu.make_async_copy(ys_hbm.at[pl.ds(0, half * rt), :],
                          buf.at[pl.ds(pl.multiple_of(slot * half * rt, rt), half * rt), :], sem.at[slot]).wait()
    y = s_ref[...].astype(F32)
    w = w_ref[...]
    for k in range(MOE_TOPK):
        y = y + w[:, k:k + 1] * _unpack_rows(_load_rows(buf, slot * half + k * tm, tm, rt))
    o_ref[...] = x_ref[...] + g_ref[0] * (_rms(y) * nw_ref[...])


def _combine(dest_flat, ys, w_tok, shared, x2, nw, mod3, gate_row, seq):
    T, D = x2.shape
    rt = D // 2 // LANES
    tm = 256
    per_b = seq // tm
    n_steps = T // tm
    return pl.pallas_call(
        functools.partial(_combine_kernel, rt=rt),
        grid=(n_steps,),
        in_specs=[pl.BlockSpec((tm * TOPK_PAD,), lambda i: (i,), memory_space=pltpu.SMEM),
                  pl.BlockSpec((tm * TOPK_PAD,), lambda i: (jnp.minimum(i + 1, n_steps - 1),), memory_space=pltpu.SMEM),
                  pl.BlockSpec(memory_space=pl.ANY),
                  pl.BlockSpec((tm, TOPK_PAD), lambda i: (i, 0)),
                  pl.BlockSpec((tm, D), lambda i: (i, 0)),
                  pl.BlockSpec((tm, D), lambda i: (i, 0)),
                  pl.BlockSpec((1, D), lambda i: (0, 0)),
                  pl.BlockSpec((1, 1, D), lambda i: (gate_row(i // per_b), 0, 0))],
        out_specs=pl.BlockSpec((tm, D), lambda i: (i, 0)),
        out_shape=jax.ShapeDtypeStruct((T, D), F32),
        scratch_shapes=[pltpu.VMEM((2 * MOE_TOPK * tm * rt, LANES), U32), pltpu.SemaphoreType.DMA((2,))],
        compiler_params=_cparams(("arbitrary",)),
        name="moe_combine",
    )(dest_flat, dest_flat, ys, w_tok, shared, x2, nw, mod3)


def _moe_ffn(x2, nw_pre, nw_post, mod3, rows, router_w, router_bias, w_gu, w_down, sh_gu, sh_down, layer, seq):
    T, D = x2.shape
    n_exp = router_w.shape[1]
    rw_hi, rw_lo = _split_bf16(router_w.T)
    h32, shared, idx, w, rank, cnt = _moe_pre(
        x2, nw_pre, mod3, rows(4), rows(3), sh_gu.astype(BF16), sh_down.astype(BF16),
        rw_hi, rw_lo, router_bias.reshape(n_exp, 1), seq)
    counts = cnt[:, 0].astype(I32)
    padded = (counts + MOE_ROWS - 1) // MOE_ROWS * MOE_ROWS
    pad_end = jnp.cumsum(padded)
    pad_start = pad_end - padded
    n_blocks = -(-(T * MOE_TOPK + n_exp * (MOE_ROWS - 1)) // MOE_ROWS)
    start_of = jnp.sum(jnp.where(idx[:, :, None] == jnp.arange(n_exp, dtype=I32), pad_start, 0), axis=-1)
    dest = jnp.where(jnp.arange(TOPK_PAD)[:, None] < MOE_TOPK, start_of + rank, 0)
    dest_flat = dest.T.reshape(T * TOPK_PAD).astype(I32)
    blk_start = jnp.arange(n_blocks, dtype=I32) * MOE_ROWS
    blk_e = jnp.minimum(jnp.sum((pad_end[None, :] <= blk_start[:, None]).astype(I32), axis=1), n_exp - 1)
    n_used = (pad_end[-1:] // MOE_ROWS).astype(I32)
    e_ids = jnp.arange(n_exp, dtype=I32)
    owner = jnp.where(padded > 0, e_ids, n_exp)
    later = lax.cummin(jnp.concatenate([owner[1:], jnp.full((1,), n_exp, I32)]), reverse=True)
    next_e = jnp.where(later < n_exp, later, jnp.max(jnp.where(padded > 0, e_ids, 0)))
    nxt_e = jnp.sum(jnp.where(blk_e[:, None] == e_ids[None, :], next_e[None, :], 0), axis=1).astype(I32)
    xs = _dispatch(dest_flat, h32, (pad_start + counts).astype(I32), (padded - counts).astype(I32), n_used, T, n_blocks)
    ys = _gmm(blk_e, n_used, nxt_e, xs, w_gu, w_down, layer)
    return _combine(dest_flat, ys, w.T, shared, x2, nw_post, mod3, rows(5), seq)


def kernel(x, c, ada_w, ada_b, norm_w, gla_w_in, gla_w_gk_up, gla_b_gk, gla_norm_w, gla_w_out,
           moba_w_qkv, moba_w_out, router_w, router_bias, moe_w_gate_up, moe_w_down,
           shared_w_gate_up, shared_w_down):
    B, S, D = x.shape
    L = ada_w.shape[0]
    T = B * S
    key_dim = gla_w_gk_up.shape[2]
    val_dim = gla_w_out.shape[1]
    n_main = 2 * key_dim + 2 * val_dim

    mod3 = _ada_mod(c, ada_w, ada_b).reshape(L * B * 6, 1, D)
    cos, sin = _rope_tables(S)
    x2 = x.reshape(T, D)
    for i in range(L):
        rows = lambda part, i=i: (lambda b: (i * B + b) * 6 + part)
        nw = [norm_w[i, r].reshape(1, D) for r in range(4)]
        j = i // N_MIXERS
        if i % N_MIXERS == 0:
            w_gk = jnp.pad(gla_w_in[j, :, n_main:], ((0, 0), (0, LANES - GLA_GATE_RANK)))
            proj, gk = _norm_proj(x2, nw[0], mod3, rows(1), rows(0), gla_w_in, j, n_main, S, w_gk=w_gk)
            wup_pad = jnp.pad(gla_w_gk_up[j], ((0, LANES - GLA_GATE_RANK), (0, 0)))
            a = _gla_core(proj, gk, wup_pad, gla_b_gk[j].reshape(1, key_dim),
                          gla_norm_w[j].reshape(1, -1), B, S, key_dim, val_dim)
            w_out = gla_w_out[j]
        else:
            (qkv,) = _norm_proj(x2, nw[0], mod3, rows(1), rows(0), moba_w_qkv, j, 3 * D, S)
            a = _moba_core(qkv, cos, sin, B, S, D)
            w_out = moba_w_out[j]
        x2 = _out_proj(a, w_out.astype(BF16), x2, nw[1], mod3, rows(2), S)
        x2 = _moe_ffn(x2, nw[2], nw[3], mod3, rows, router_w[i], router_bias[i], moe_w_gate_up, moe_w_down,
                      shared_w_gate_up[i], shared_w_down[i], i, S)
    return x2.reshape(B, S, D)
```

```python
import functools

import jax
import jax.numpy as jnp
from jax import lax
from jax.experimental import pallas as pl
from jax.experimental.pallas import tpu as pltpu

F32 = jnp.float32
BF16 = jnp.bfloat16
U32 = jnp.uint32
I32 = jnp.int32

NORM_EPS = 1e-6
N_MIXERS = 2
GLA_HEADS = 4
GLA_GATE_RANK = 16
GLA_GATE_NORMALIZER = 16.0
GLA_CHUNK = 64
MOBA_HEAD_DIM = 128
MOBA_BLOCK = 256
MOBA_TOPK = 3
ROPE_THETA = 10000.0
MOE_TOPK = 6
N_GROUPS = 8
TOPK_GROUPS = 4
ROUTED_SCALE = 2.5

LANES = 128
SUBLANES = 8
MOE_ROWS = 256
ROUTE_LANES = LANES
TOPK_PAD = 8
VMEM_LIMIT = 56 * 1024 * 1024

NEG_INF = float("-inf")
HI_MASK = 0xFFFF0000
LOG2E = 1.4426950408889634
ONES_ROWS = 16
ROW_TILE = SUBLANES


def _cparams(sem):
    return pltpu.CompilerParams(dimension_semantics=sem, vmem_limit_bytes=VMEM_LIMIT)


def _sigmoid(x):
    return 1.0 / (1.0 + jnp.exp(-x))


def _silu(x):
    return x * _sigmoid(x)


def _nt(a, b):
    return lax.dot_general(a, b, (((1,), (1,)), ((), ())), preferred_element_type=F32)


def _tn(a, b):
    return lax.dot_general(a, b, (((0,), (0,)), ((), ())), preferred_element_type=F32)


def _dot(a, b):
    return jnp.dot(a, b, preferred_element_type=F32)


def _split_bf16(x):
    hi = x.astype(BF16)
    lo = (x - hi.astype(F32)).astype(BF16)
    return hi, lo


def _pack_rows(y):
    w = y.shape[1] // 2
    bits = pltpu.bitcast(y.astype(BF16).astype(F32), U32)
    return (bits[:, :w] >> 16) | (bits[:, w:] & jnp.uint32(HI_MASK))


def _unpack_rows(w):
    lo = pltpu.bitcast(w << 16, F32)
    hi = pltpu.bitcast(w & jnp.uint32(HI_MASK), F32)
    return jnp.concatenate([lo, hi], axis=1)


def _store_rows(ref, first, packed):
    n, w = packed.shape
    rt = w // LANES
    for s in range(rt):
        ref[pl.ds(first * rt + s, n, stride=rt), :] = packed[:, s * LANES:(s + 1) * LANES]


def _load_rows(ref, first, n, rt):
    return jnp.concatenate([ref[pl.ds(first * rt + s, n, stride=rt), :] for s in range(rt)], axis=1)


def _rms(x):
    return x * lax.rsqrt(jnp.mean(x * x, axis=-1, keepdims=True) + NORM_EPS)


def _ada_kernel(c_ref, w_ref, b_ref, o_ref):
    cond = _silu(c_ref[...]).astype(BF16)
    o_ref[0] = _dot(cond, w_ref[0].astype(BF16)) + b_ref[0]


def _ada_mod(c, ada_w, ada_b):
    L, D, N = ada_w.shape
    B = c.shape[0]
    tn = 1024
    return pl.pallas_call(
        _ada_kernel,
        grid=(L, N // tn),
        in_specs=[pl.BlockSpec((B, D), lambda l, j: (0, 0)),
                  pl.BlockSpec((1, D, tn), lambda l, j: (l, 0, j)),
                  pl.BlockSpec((1, 1, tn), lambda l, j: (l, 0, j))],
        out_specs=pl.BlockSpec((1, B, tn), lambda l, j: (l, 0, j)),
        out_shape=jax.ShapeDtypeStruct((L, B, N), F32),
        compiler_params=_cparams(("arbitrary", "arbitrary")),
        name="ada_mod",
    )(c, ada_w, ada_b.reshape(L, 1, N))


def _prenorm(x, nw, scale, shift):
    return _rms(x) * nw * (1.0 + scale) + shift


def _normproj_kernel(x_ref, nw_ref, sc_ref, sh_ref, w_ref, *rest, with_gk):
    if with_gk:
        wg_ref, o_ref, og_ref, h_scr = rest
    else:
        o_ref, h_scr = rest

    @pl.when(pl.program_id(1) == 0)
    def _():
        h = _prenorm(x_ref[...], nw_ref[...], sc_ref[0], sh_ref[0])
        h_scr[...] = h.astype(BF16)
        if with_gk:
            og_ref[...] = _dot(h_scr[...], wg_ref[...].astype(BF16))

    o_ref[...] = _dot(h_scr[...], w_ref[0].astype(BF16)).astype(o_ref.dtype)


def _norm_proj(x2, nw, mod3, sc_row, sh_row, w3, widx, n_main, seq, w_gk=None):
    T, D = x2.shape
    tm = min(1024, seq)
    tn = 1024 if n_main % 1024 == 0 else 512
    per_b = seq // tm
    mod_spec = lambda row: pl.BlockSpec((1, 1, D), lambda i, j: (row(i // per_b), 0, 0))
    in_specs = [pl.BlockSpec((tm, D), lambda i, j: (i, 0)),
                pl.BlockSpec((1, D), lambda i, j: (0, 0)),
                mod_spec(sc_row), mod_spec(sh_row),
                pl.BlockSpec((1, D, tn), lambda i, j: (widx, 0, j))]
    out_specs = [pl.BlockSpec((tm, tn), lambda i, j: (i, j))]
    out_shape = [jax.ShapeDtypeStruct((T, n_main), BF16)]
    args = [x2, nw, mod3, mod3, w3]
    if w_gk is not None:
        in_specs.append(pl.BlockSpec((D, LANES), lambda i, j: (0, 0)))
        out_specs.append(pl.BlockSpec((tm, LANES), lambda i, j: (i, 0)))
        out_shape.append(jax.ShapeDtypeStruct((T, LANES), F32))
        args.append(w_gk)
    return pl.pallas_call(
        functools.partial(_normproj_kernel, with_gk=w_gk is not None),
        grid=(T // tm, n_main // tn),
        in_specs=in_specs, out_specs=out_specs, out_shape=out_shape,
        scratch_shapes=[pltpu.VMEM((tm, D), BF16)],
        compiler_params=_cparams(("arbitrary", "arbitrary")),
        name="norm_proj",
    )(*args)


def _outproj_kernel(a_ref, w_ref, x_ref, nw_ref, g_ref, o_ref):
    y = _dot(a_ref[...], w_ref[...])
    o_ref[...] = x_ref[...] + g_ref[0] * (_rms(y) * nw_ref[...])


def _out_proj(a, w_bf, x2, nw, mod3, gate_row, seq):
    T, D = x2.shape
    tm = min(512, seq)
    per_b = seq // tm
    return pl.pallas_call(
        _outproj_kernel,
        grid=(T // tm,),
        in_specs=[pl.BlockSpec((tm, a.shape[1]), lambda i: (i, 0)),
                  pl.BlockSpec(w_bf.shape, lambda i: (0, 0)),
                  pl.BlockSpec((tm, D), lambda i: (i, 0)),
                  pl.BlockSpec((1, D), lambda i: (0, 0)),
                  pl.BlockSpec((1, 1, D), lambda i: (gate_row(i // per_b), 0, 0))],
        out_specs=pl.BlockSpec((tm, D), lambda i: (i, 0)),
        out_shape=jax.ShapeDtypeStruct((T, D), F32),
        compiler_params=_cparams(("arbitrary",)),
        name="out_proj",
    )(a, w_bf, x2, nw, mod3)


def _gla_kernel(q_ref, k_ref, v_ref, g_ref, gk_ref, wup_ref, bgk_ref, nw_ref, o_ref, st_ref, *, n_chunk, dk):
    L = GLA_CHUNK

    @pl.when(pl.program_id(2) == 0)
    def _():
        st_ref[...] = jnp.zeros_like(st_ref)

    z = _dot(gk_ref[...], wup_ref[...]) + bgk_ref[...]
    log_a = (jnp.minimum(z, 0.0) - jnp.log(1.0 + jnp.exp(-jnp.abs(z)))) / GLA_GATE_NORMALIZER
    row = lax.broadcasted_iota(I32, (L, L), 0)
    col = lax.broadcasted_iota(I32, (L, L), 1)
    causal = row >= col
    tri = jnp.where(causal, 1.0, 0.0).astype(BF16)
    scale = dk ** -0.5
    for c in range(n_chunk):
        sl = pl.ds(c * L, L)
        la_hi, la_lo = _split_bf16(log_a[c * L:(c + 1) * L])
        b = _dot(tri, la_hi) + _dot(tri, la_lo)
        b_last = b[L - 1:L]
        q = q_ref[sl, :].astype(F32) * scale
        k = k_ref[sl, :].astype(F32)
        v = v_ref[sl, :]
        qe = (q * jnp.exp(b)).astype(BF16)
        ke = (k * jnp.exp(-b)).astype(BF16)
        kt = (k * jnp.exp(b_last - b)).astype(BF16)
        att = jnp.where(causal, _nt(qe, ke), 0.0)
        st = st_ref[...]
        o = _dot(att.astype(BF16), v) + _nt(qe, st.astype(BF16))
        st_ref[...] = st * jnp.exp(b_last) + _tn(v, kt)
        g = g_ref[sl, :].astype(F32)
        o_ref[sl, :] = (_rms(o) * nw_ref[...] * _silu(g)).astype(o_ref.dtype)


def _gla_core(proj, gk, wup_pad, bgk, gnw, batch, seq, key_dim, val_dim):
    T = proj.shape[0]
    H = GLA_HEADS
    dk, dv = key_dim // H, val_dim // H
    R = min(512, seq)
    nr = seq // R
    row = lambda b, h, r: b * nr + r
    return pl.pallas_call(
        functools.partial(_gla_kernel, n_chunk=R // GLA_CHUNK, dk=dk),
        grid=(batch, H, nr),
        in_specs=[pl.BlockSpec((R, dk), lambda b, h, r: (row(b, h, r), h)),
                  pl.BlockSpec((R, dk), lambda b, h, r: (row(b, h, r), H + h)),
                  pl.BlockSpec((R, dv), lambda b, h, r: (row(b, h, r), 2 * key_dim // dv + h)),
                  pl.BlockSpec((R, dv), lambda b, h, r: (row(b, h, r), (2 * key_dim + val_dim) // dv + h)),
                  pl.BlockSpec((R, LANES), lambda b, h, r: (row(b, h, r), 0)),
                  pl.BlockSpec((LANES, dk), lambda b, h, r: (0, h)),
                  pl.BlockSpec((1, dk), lambda b, h, r: (0, h)),
                  pl.BlockSpec((1, dv), lambda b, h, r: (0, 0))],
        out_specs=pl.BlockSpec((R, dv), lambda b, h, r: (row(b, h, r), h)),
        out_shape=jax.ShapeDtypeStruct((T, val_dim), BF16),
        scratch_shapes=[pltpu.VMEM((dv, dk), F32)],
        compiler_params=_cparams(("arbitrary", "arbitrary", "arbitrary")),
        name="gla_core",
    )(proj, proj, proj, proj, gk, wup_pad, bgk, gnw)


def _moba_kernel(q_ref, k_ref, v_ref, cos_ref, sin_ref, o_ref, s_scr, *, nb):
    blk, hd = MOBA_BLOCK, MOBA_HEAD_DIM
    S = nb * blk
    cos, sin = cos_ref[...], sin_ref[...]

    def rope(t_ref):
        t = t_ref[...].astype(F32)
        return t * cos + pltpu.roll(t, hd // 2, 1) * sin

    q = rope(q_ref) * (hd ** -0.5 * LOG2E)
    k = rope(k_ref)
    qb, kb = q.astype(BF16), k.astype(BF16)
    vT = jnp.concatenate([v_ref[...].astype(F32).T, jnp.ones((ONES_ROWS, S), F32)], axis=0).astype(BF16)

    n_sel = min(MOBA_TOPK, nb - 1)
    if n_sel > 0:
        kmean = jnp.concatenate(
            [jnp.mean(k[j * blk:(j + 1) * blk], axis=0, keepdims=True) for j in range(nb)], axis=0)
        jrow = lax.broadcasted_iota(I32, (nb, S), 0)
        qblk = lax.shift_right_logical(lax.broadcasted_iota(I32, (nb, S), 1), blk.bit_length() - 1)
        past = jrow < qblk
        gate = jnp.where(past, _nt(kmean, q), NEG_INF)
        rank = jnp.zeros((nb, S), F32)
        for j2 in range(nb):
            r = gate[j2:j2 + 1]
            beats = jnp.where(r > gate, 1.0, jnp.where((r == gate) & (j2 < jrow), 1.0, 0.0))
            rank = rank + beats
        sel = jnp.where(past & (rank < n_sel), 1.0, 0.0)

    kk = lax.broadcasted_iota(I32, (blk, blk), 0)
    qq = lax.broadcasted_iota(I32, (blk, blk), 1)
    for i in range(nb):
        qi = qb[i * blk:(i + 1) * blk]
        n_keys = (i + 1) if n_sel > 0 else 1
        s = jnp.where(kk <= qq, _nt(kb[i * blk:(i + 1) * blk], qi), NEG_INF)
        s_scr[pl.ds(0, blk), :] = s
        m = jnp.max(s, axis=0, keepdims=True)
        for j in range(n_keys - 1):
            keep = sel[j:j + 1, i * blk:(i + 1) * blk] > 0.0
            s = jnp.where(keep, _nt(kb[j * blk:(j + 1) * blk], qi), NEG_INF)
            s_scr[pl.ds((j + 1) * blk, blk), :] = s
            m = jnp.maximum(m, jnp.max(s, axis=0, keepdims=True))
        acc = jnp.zeros((hd + ONES_ROWS, blk), F32)
        for t in range(n_keys):
            j = i if t == 0 else t - 1
            p = jnp.exp2(s_scr[pl.ds(t * blk, blk), :] - m).astype(BF16)
            acc = acc + _dot(vT[:, j * blk:(j + 1) * blk], p)
        o_ref[pl.ds(i * blk, blk), :] = (acc[:hd] / acc[hd:hd + 1]).T.astype(o_ref.dtype)


def _moba_core(qkv, cos, sin, batch, seq, d_model):
    T = qkv.shape[0]
    hd = MOBA_HEAD_DIM
    H = d_model // hd
    nb = seq // MOBA_BLOCK
    return pl.pallas_call(
        functools.partial(_moba_kernel, nb=nb),
        grid=(batch, H),
        in_specs=[pl.BlockSpec((seq, hd), lambda b, h: (b, h)),
                  pl.BlockSpec((seq, hd), lambda b, h: (b, H + h)),
                  pl.BlockSpec((seq, hd), lambda b, h: (b, 2 * H + h)),
                  pl.BlockSpec((seq, hd), lambda b, h: (0, 0)),
                  pl.BlockSpec((seq, hd), lambda b, h: (0, 0))],
        out_specs=pl.BlockSpec((seq, hd), lambda b, h: (b, h)),
        out_shape=jax.ShapeDtypeStruct((T, d_model), BF16),
        scratch_shapes=[pltpu.VMEM((seq, MOBA_BLOCK), F32)],
        compiler_params=_cparams(("arbitrary", "arbitrary")),
        name="moba_core",
    )(qkv, qkv, qkv, cos, sin)


def _rope_tables(seq):
    hd = MOBA_HEAD_DIM
    half = hd // 2
    inv = jnp.power(ROPE_THETA, -jnp.arange(half, dtype=F32) * 2.0 / hd)
    ang = jnp.arange(seq).astype(F32)[:, None] * inv[None, :]
    cos, sin = jnp.cos(ang), jnp.sin(ang)
    return jnp.concatenate([cos, cos], axis=-1), jnp.concatenate([-sin, sin], axis=-1)


def _route_chunk(lt, bias, carry, n_exp):
    C = lt.shape[1]
    per_g = n_exp // N_GROUPS
    scores = _sigmoid(lt)
    biased = scores + bias
    sub = lax.broadcasted_iota(I32, (per_g, C), 0).astype(F32)
    gs_rows, groups = [], []
    for g in range(N_GROUPS):
        xg = biased[g * per_g:(g + 1) * per_g]
        groups.append(xg)
        m1 = jnp.max(xg, axis=0, keepdims=True)
        i1 = jnp.min(jnp.where(xg == m1, sub, float(per_g)), axis=0, keepdims=True)
        m2 = jnp.max(jnp.where(sub == i1, NEG_INF, xg), axis=0, keepdims=True)
        gs_rows.append(m1 + m2)
    gs = jnp.concatenate(gs_rows, axis=0)
    gidx = lax.broadcasted_iota(I32, (N_GROUPS, C), 0)
    grank = jnp.zeros((N_GROUPS, C), F32)
    for g2 in range(N_GROUPS):
        r = gs[g2:g2 + 1]
        grank = grank + jnp.where(r > gs, 1.0, jnp.where((r == gs) & (g2 < gidx), 1.0, 0.0))
    gkeep = grank < TOPK_GROUPS
    cur = jnp.concatenate(
        [jnp.where(gkeep[g:g + 1], groups[g], NEG_INF) for g in range(N_GROUPS)], axis=0)
    eidx = lax.broadcasted_iota(I32, (n_exp, C), 0).astype(F32)
    sels, idxs, ws = [], [], []
    for _ in range(MOE_TOPK):
        m = jnp.max(cur, axis=0, keepdims=True)
        ik = jnp.min(jnp.where(cur == m, eidx, float(n_exp)), axis=0, keepdims=True)
        sk = eidx == ik
        sels.append(sk)
        idxs.append(ik)
        ws.append(jnp.sum(jnp.where(sk, scores, 0.0), axis=0, keepdims=True))
        cur = jnp.where(sk, NEG_INF, cur)
    wsum = ws[0]
    for w in ws[1:]:
        wsum = wsum + w
    msel = jnp.zeros((n_exp, C), F32)
    for sk in sels:
        msel = msel + jnp.where(sk, 1.0, 0.0)
    msel_b = msel.astype(BF16)
    tr = lax.broadcasted_iota(I32, (C, C), 0)
    tc = lax.broadcasted_iota(I32, (C, C), 1)
    before = jnp.where(tr < tc, 1.0, 0.0).astype(BF16)
    prefix = carry + _dot(msel_b, before)
    ranks = [jnp.sum(jnp.where(sk, prefix, 0.0), axis=0, keepdims=True) for sk in sels]
    carry = carry + _dot(msel_b, jnp.ones((C, C), BF16))
    pad = [jnp.zeros((TOPK_PAD - MOE_TOPK, C), F32)]
    idx_rows = jnp.concatenate(idxs + pad, axis=0).astype(I32)
    w_rows = jnp.concatenate([w / wsum * ROUTED_SCALE for w in ws] + pad, axis=0)
    rank_rows = jnp.concatenate(ranks + pad, axis=0).astype(I32)
    return idx_rows, w_rows, rank_rows, carry


def _moe_pre_kernel(x_ref, nw_ref, sc_ref, sh_ref, rwh_ref, rwl_ref, bias_ref,
                    h_ref, idx_ref, w_ref, rank_ref, cnt_ref, carry_ref, *, n_exp):
    @pl.when(pl.program_id(0) == 0)
    def _():
        carry_ref[...] = jnp.zeros_like(carry_ref)

    h = _prenorm(x_ref[...], nw_ref[...], sc_ref[0], sh_ref[0])
    h_hi, h_lo = _split_bf16(h)
    _store_rows(h_ref, 0, _pack_rows(h))
    logits = _nt(rwh_ref[...], h_hi) + _nt(rwh_ref[...], h_lo) + _nt(rwl_ref[...], h_hi)
    carry = carry_ref[...]
    for c in range(x_ref.shape[0] // ROUTE_LANES):
        cols = pl.ds(c * ROUTE_LANES, ROUTE_LANES)
        i_rows, w_rows, r_rows, carry = _route_chunk(
            logits[:, c * ROUTE_LANES:(c + 1) * ROUTE_LANES], bias_ref[...], carry, n_exp)
        idx_ref[:, cols] = i_rows
        w_ref[:, cols] = w_rows
        rank_ref[:, cols] = r_rows
    carry_ref[...] = carry
    cnt_ref[...] = carry


def _moe_pre(x2, nw, mod3, sc_row, sh_row, rw_hi, rw_lo, bias_col, seq):
    T, D = x2.shape
    n_exp = rw_hi.shape[0]
    rt = D // 2 // LANES
    tm = min(512, seq)
    per_b = seq // tm
    full = lambda a: pl.BlockSpec(a.shape, lambda i: (0,) * a.ndim)
    tok_rows = pl.BlockSpec((TOPK_PAD, tm), lambda i: (0, i))
    return pl.pallas_call(
        functools.partial(_moe_pre_kernel, n_exp=n_exp),
        grid=(T // tm,),
        in_specs=[pl.BlockSpec((tm, D), lambda i: (i, 0)),
                  pl.BlockSpec((1, D), lambda i: (0, 0)),
                  pl.BlockSpec((1, 1, D), lambda i: (sc_row(i // per_b), 0, 0)),
                  pl.BlockSpec((1, 1, D), lambda i: (sh_row(i // per_b), 0, 0)),
                  full(rw_hi), full(rw_lo), full(bias_col)],
        out_specs=[pl.BlockSpec((tm * rt, LANES), lambda i: (i, 0)),
                   tok_rows, tok_rows, tok_rows,
                   pl.BlockSpec((n_exp, ROUTE_LANES), lambda i: (0, 0))],
        out_shape=[jax.ShapeDtypeStruct((T * rt, LANES), U32),
                   jax.ShapeDtypeStruct((TOPK_PAD, T), I32),
                   jax.ShapeDtypeStruct((TOPK_PAD, T), F32),
                   jax.ShapeDtypeStruct((TOPK_PAD, T), I32),
                   jax.ShapeDtypeStruct((n_exp, ROUTE_LANES), F32)],
        scratch_shapes=[pltpu.VMEM((n_exp, ROUTE_LANES), F32)],
        compiler_params=_cparams(("arbitrary",)),
        name="moe_pre",
    )(x2, nw, mod3, mod3, rw_hi, rw_lo, bias_col)


def _tile_rows(ref, tile, rt):
    return ref.at[pl.ds(pl.multiple_of(tile * rt, rt), rt), :]


def _zero_fill(fill_start_ref, fill_n_ref, n_used_ref, zbuf, xs_out, zsem, *, rt, n_exp, n_blocks, wait):
    def copy(first_row, n_rows):
        cp = pltpu.make_async_copy(zbuf.at[pl.ds(0, n_rows * rt), :],
                                   xs_out.at[pl.ds(pl.multiple_of(first_row * rt, rt), n_rows * rt), :], zsem)
        cp.wait() if wait else cp.start()

    def per_expert(e, c):
        start, n = fill_start_ref[e], fill_n_ref[e]
        p = MOE_ROWS // 2
        while p >= 1:
            @pl.when((n & p) != 0)
            def _(p=p):
                copy(start + (n & ~(2 * p - 1)), p)
            p //= 2
        return c

    lax.fori_loop(0, n_exp, per_expert, 0)

    def per_tail_block(b, c):
        copy(b * MOE_ROWS, MOE_ROWS)
        return c

    lax.fori_loop(n_used_ref[0], n_blocks, per_tail_block, 0)


def _dispatch_kernel(fill_start_ref, fill_n_ref, n_used_ref, dest_ref, h_ref, wgu_ref, wd_ref, xs_out, s_ref,
                     zbuf, sem, zsem, *, rt, n_exp, n_blocks, d_ff):
    tm = h_ref.shape[0] // rt
    fill = functools.partial(_zero_fill, fill_start_ref, fill_n_ref, n_used_ref, zbuf, xs_out, zsem,
                             rt=rt, n_exp=n_exp, n_blocks=n_blocks)

    @pl.when(pl.program_id(0) == 0)
    def _():
        zbuf[...] = jnp.zeros_like(zbuf)
        fill(wait=False)

    def issue(t, c):
        for k in range(MOE_TOPK):
            pltpu.make_async_copy(_tile_rows(h_ref, t, rt),
                                  _tile_rows(xs_out, dest_ref[t * TOPK_PAD + k], rt), sem).start(priority=k % 2)
        return c

    lax.fori_loop(0, tm, issue, 0)
    x = _unpack_rows(_load_rows(h_ref, 0, tm, rt)).astype(BF16)
    gu = _dot(x, wgu_ref[...])
    act = (_silu(gu[:, :d_ff]) * gu[:, d_ff:]).astype(BF16)
    s_ref[...] = _dot(act, wd_ref[...]).astype(s_ref.dtype)
    for _ in range(MOE_TOPK):
        pltpu.make_async_copy(h_ref, xs_out.at[pl.ds(0, tm * rt), :], sem).wait()

    @pl.when(pl.program_id(0) == 0)
    def _():
        fill(wait=True)


def _dispatch(dest_flat, h32, wgu_bf, wd_bf, fill_start, fill_n, n_used, n_tok, n_blocks):
    rt = h32.shape[0] // n_tok
    d_ff, D = wd_bf.shape
    tm = 256
    return pl.pallas_call(
        functools.partial(_dispatch_kernel, rt=rt, n_exp=fill_n.shape[0], n_blocks=n_blocks, d_ff=d_ff),
        grid_spec=pltpu.PrefetchScalarGridSpec(
            num_scalar_prefetch=3,
            grid=(n_tok // tm,),
            in_specs=[pl.BlockSpec((tm * TOPK_PAD,), lambda i, *_: (i,), memory_space=pltpu.SMEM),
                      pl.BlockSpec((tm * rt, LANES), lambda i, *_: (i, 0)),
                      pl.BlockSpec(wgu_bf.shape, lambda i, *_: (0, 0)),
                      pl.BlockSpec(wd_bf.shape, lambda i, *_: (0, 0))],
            out_specs=[pl.BlockSpec(memory_space=pl.ANY),
                       pl.BlockSpec((tm, D), lambda i, *_: (i, 0))],
            scratch_shapes=[pltpu.VMEM((MOE_ROWS * rt, LANES), U32),
                            pltpu.SemaphoreType.DMA(()), pltpu.SemaphoreType.DMA(())]),
        out_shape=[jax.ShapeDtypeStruct((n_blocks * MOE_ROWS * rt, LANES), U32),
                   jax.ShapeDtypeStruct((n_tok, D), BF16)],
        compiler_params=_cparams(("arbitrary",)),
        name="moe_dispatch",
    )(fill_start, fill_n, n_used, dest_flat, h32, wgu_bf, wd_bf)


def _gmm_kernel(blk_e_ref, n_used_ref, nxt_e_ref, xs_ref, wgu_ref, wd_ref, ys_ref, wgu_bf, wd_bf, act_scr, *, d_ff):
    del nxt_e_ref
    i = pl.program_id(0)
    nu = n_used_ref[0]

    def expert(b):
        return blk_e_ref[jnp.clip(b, 0, nu - 1)]

    s1 = i < nu
    s2 = (i >= 1) & (i <= nu)

    @pl.when(s1 & ((i == 0) | (expert(i) != expert(i - 1))))
    def _():
        wgu_bf[...] = wgu_ref[0, 0].astype(BF16)

    @pl.when(s2 & ((i == 1) | (expert(i - 1) != expert(i - 2))))
    def _():
        wd_bf[...] = wd_ref[0, 0].astype(BF16)

    def stage1():
        rt = xs_ref.shape[0] // MOE_ROWS
        x = _unpack_rows(_load_rows(xs_ref, 0, MOE_ROWS, rt)).astype(BF16)
        gu = _dot(x, wgu_bf[...])
        act_scr[i % 2] = (_silu(gu[:, :d_ff]) * gu[:, d_ff:]).astype(BF16)

    def stage2():
        _store_rows(ys_ref, 0, _pack_rows(_dot(act_scr[(i + 1) % 2], wd_bf[...])))

    @pl.when(s1 & s2)
    def _():
        stage2()
        stage1()

    @pl.when(s1 & jnp.logical_not(s2))
    def _():
        stage1()

    @pl.when(s2 & jnp.logical_not(s1))
    def _():
        stage2()

    @pl.when(jnp.logical_not(s1 | s2))
    def _():
        ys_ref[...] = jnp.zeros_like(ys_ref)


def _gmm(blk_e, n_used, nxt_e, xs, w_gu, w_down, layer):
    _, _, D, two_ff = w_gu.shape
    d_ff = two_ff // 2
    rt = D // 2 // LANES
    n_blocks = xs.shape[0] // (MOE_ROWS * rt)

    def live(b, nu):
        return jnp.clip(b, 0, nu[0] - 1)

    def ahead(b, be, nu, nx):
        return jnp.where(b < 0, be[0], nx[live(b, nu)])

    return pl.pallas_call(
        functools.partial(_gmm_kernel, d_ff=d_ff),
        grid_spec=pltpu.PrefetchScalarGridSpec(
            num_scalar_prefetch=3,
            grid=(n_blocks + 1,),
            in_specs=[pl.BlockSpec((MOE_ROWS * rt, LANES), lambda i, be, nu, nx: (live(i, nu), 0)),
                      pl.BlockSpec((1, 1, D, two_ff), lambda i, be, nu, nx: (layer, ahead(i - 1, be, nu, nx), 0, 0)),
                      pl.BlockSpec((1, 1, d_ff, D), lambda i, be, nu, nx: (layer, ahead(i - 2, be, nu, nx), 0, 0))],
            out_specs=pl.BlockSpec((MOE_ROWS * rt, LANES), lambda i, be, nu, nx: (jnp.maximum(i - 1, 0), 0)),
            scratch_shapes=[pltpu.VMEM((D, two_ff), BF16), pltpu.VMEM((d_ff, D), BF16),
                            pltpu.VMEM((2, MOE_ROWS, d_ff), BF16)]),
        out_shape=jax.ShapeDtypeStruct(xs.shape, U32),
        compiler_params=_cparams(("arbitrary",)),
        name="moe_gmm",
    )(blk_e, n_used, nxt_e, xs, w_gu, w_down)


def _combine_kernel(dest_ref, dest_next_ref, ys_hbm, w_ref, s_ref, x_ref, nw_ref, g_ref, o_ref, buf, sem, *, rt):
    tm = x_ref.shape[0]
    i = pl.program_id(0)
    half = MOE_TOPK * tm
    slot = i % 2

    def gather(d_ref, into):
        def issue(t, c):
            for k in range(MOE_TOPK):
                pltpu.make_async_copy(_tile_rows(ys_hbm, d_ref[t * TOPK_PAD + k], rt),
                                      _tile_rows(buf, into * half + k * tm + t, rt),
                                      sem.at[into]).start(priority=k % 2)
            return c
        lax.fori_loop(0, tm, issue, 0)

    @pl.when(i == 0)
    def _():
        gather(dest_ref, 0)

    @pl.when(i + 1 < pl.num_programs(0))
    def _():
        gather(dest_next_ref, 1 - slot)

    pltpu.make_async_copy(ys_hbm.at[pl.ds(0, half * rt), :],
                          buf.at[pl.ds(pl.multiple_of(slot * half * rt, rt), half * rt), :], sem.at[slot]).wait()
    y = s_ref[...].astype(F32)
    w = w_ref[...]
    for k in range(MOE_TOPK):
        y = y + w[:, k:k + 1] * _unpack_rows(_load_rows(buf, slot * half + k * tm, tm, rt))
    o_ref[...] = x_ref[...] + g_ref[0] * (_rms(y) * nw_ref[...])


def _combine(dest_flat, ys, w_tok, shared, x2, nw, mod3, gate_row, seq):
    T, D = x2.shape
    rt = D // 2 // LANES
    tm = 256
    per_b = seq // tm
    n_steps = T // tm
    return pl.pallas_call(
        functools.partial(_combine_kernel, rt=rt),
        grid=(n_steps,),
        in_specs=[pl.BlockSpec((tm * TOPK_PAD,), lambda i: (i,), memory_space=pltpu.SMEM),
                  pl.BlockSpec((tm * TOPK_PAD,), lambda i: (jnp.minimum(i + 1, n_steps - 1),), memory_space=pltpu.SMEM),
                  pl.BlockSpec(memory_space=pl.ANY),
                  pl.BlockSpec((tm, TOPK_PAD), lambda i: (i, 0)),
                  pl.BlockSpec((tm, D), lambda i: (i, 0)),
                  pl.BlockSpec((tm, D), lambda i: (i, 0)),
                  pl.BlockSpec((1, D), lambda i: (0, 0)),
                  pl.BlockSpec((1, 1, D), lambda i: (gate_row(i // per_b), 0, 0))],
        out_specs=pl.BlockSpec((tm, D), lambda i: (i, 0)),
        out_shape=jax.ShapeDtypeStruct((T, D), F32),
        scratch_shapes=[pltpu.VMEM((2 * MOE_TOPK * tm * rt, LANES), U32), pltpu.SemaphoreType.DMA((2,))],
        compiler_params=_cparams(("arbitrary",)),
        name="moe_combine",
    )(dest_flat, dest_flat, ys, w_tok, shared, x2, nw, mod3)


def _moe_ffn(x2, nw_pre, nw_post, mod3, rows, router_w, router_bias, w_gu, w_down, sh_gu, sh_down, layer, seq):
    T, D = x2.shape
    n_exp = router_w.shape[1]
    rw_hi, rw_lo = _split_bf16(router_w.T)
    h32, idx, w, rank, cnt = _moe_pre(
        x2, nw_pre, mod3, rows(4), rows(3), rw_hi, rw_lo, router_bias.reshape(n_exp, 1), seq)
    counts = cnt[:, 0].astype(I32)
    padded = (counts + MOE_ROWS - 1) // MOE_ROWS * MOE_ROWS
    pad_end = jnp.cumsum(padded)
    pad_start = pad_end - padded
    n_blocks = -(-(T * MOE_TOPK + n_exp * (MOE_ROWS - 1)) // MOE_ROWS)
    start_of = jnp.sum(jnp.where(idx[:, :, None] == jnp.arange(n_exp, dtype=I32), pad_start, 0), axis=-1)
    dest = jnp.where(jnp.arange(TOPK_PAD)[:, None] < MOE_TOPK, start_of + rank, 0)
    dest_flat = dest.T.reshape(T * TOPK_PAD).astype(I32)
    blk_start = jnp.arange(n_blocks, dtype=I32) * MOE_ROWS
    blk_e = jnp.minimum(jnp.sum((pad_end[None, :] <= blk_start[:, None]).astype(I32), axis=1), n_exp - 1)
    n_used = (pad_end[-1:] // MOE_ROWS).astype(I32)
    e_ids = jnp.arange(n_exp, dtype=I32)
    owner = jnp.where(padded > 0, e_ids, n_exp)
    later = lax.cummin(jnp.concatenate([owner[1:], jnp.full((1,), n_exp, I32)]), reverse=True)
    next_e = jnp.where(later < n_exp, later, jnp.max(jnp.where(padded > 0, e_ids, 0)))
    nxt_e = jnp.sum(jnp.where(blk_e[:, None] == e_ids[None, :], next_e[None, :], 0), axis=1).astype(I32)
    xs, shared = _dispatch(dest_flat, h32, sh_gu.astype(BF16), sh_down.astype(BF16),
                           (pad_start + counts).astype(I32), (padded - counts).astype(I32), n_used, T, n_blocks)
    ys = _gmm(blk_e, n_used, nxt_e, xs, w_gu, w_down, layer)
    return _combine(dest_flat, ys, w.T, shared, x2, nw_post, mod3, rows(5), seq)


def kernel(x, c, ada_w, ada_b, norm_w, gla_w_in, gla_w_gk_up, gla_b_gk, gla_norm_w, gla_w_out,
           moba_w_qkv, moba_w_out, router_w, router_bias, moe_w_gate_up, moe_w_down,
           shared_w_gate_up, shared_w_down):
    B, S, D = x.shape
    L = ada_w.shape[0]
    T = B * S
    key_dim = gla_w_gk_up.shape[2]
    val_dim = gla_w_out.shape[1]
    n_main = 2 * key_dim + 2 * val_dim

    mod3 = _ada_mod(c, ada_w, ada_b).reshape(L * B * 6, 1, D)
    cos, sin = _rope_tables(S)
    x2 = x.reshape(T, D)
    for i in range(L):
        rows = lambda part, i=i: (lambda b: (i * B + b) * 6 + part)
        nw = [norm_w[i, r].reshape(1, D) for r in range(4)]
        j = i // N_MIXERS
        if i % N_MIXERS == 0:
            w_gk = jnp.pad(gla_w_in[j, :, n_main:], ((0, 0), (0, LANES - GLA_GATE_RANK)))
            proj, gk = _norm_proj(x2, nw[0], mod3, rows(1), rows(0), gla_w_in, j, n_main, S, w_gk=w_gk)
            wup_pad = jnp.pad(gla_w_gk_up[j], ((0, LANES - GLA_GATE_RANK), (0, 0)))
            a = _gla_core(proj, gk, wup_pad, gla_b_gk[j].reshape(1, key_dim),
                          gla_norm_w[j].reshape(1, -1), B, S, key_dim, val_dim)
            w_out = gla_w_out[j]
        else:
            (qkv,) = _norm_proj(x2, nw[0], mod3, rows(1), rows(0), moba_w_qkv, j, 3 * D, S)
            a = _moba_core(qkv, cos, sin, B, S, D)
            w_out = moba_w_out[j]
        x2 = _out_proj(a, w_out.astype(BF16), x2, nw[1], mod3, rows(2), S)
        x2 = _moe_ffn(x2, nw[2], nw[3], mod3, rows, router_w[i], router_bias[i], moe_w_gate_up, moe_w_down,
                      shared_w_gate_up[i], shared_w_down[i], i, S)
    return x2.reshape(B, S, D)
```
